```python
import jax, jax.numpy as jnp
from jax import lax
import numpy as np

D_MODEL = 1024
BATCH = 16
SEQ = 4096
DEPTH = 2
DEC_BATCH = 8
DEC_SEQ = 8192
PAST_LEN = 128

GRID_W = 64
N_EVEN = (DEPTH + 1) // 2
N_ODD = DEPTH // 2
D_FF = 2816
EPS = 1e-6
FOURIER_W = D_MODEL // 4
FOURIER_CH = 64
FOURIER_GROUPS = FOURIER_W // FOURIER_CH
HEAD_DIM = 64
ATTN_W = 3 * D_MODEL // 4
N_Q_HEADS = ATTN_W // HEAD_DIM
N_KV_HEADS = 4
Q_PER_KV = N_Q_HEADS // N_KV_HEADS
KV_W = N_KV_HEADS * HEAD_DIM
AB_IN_W = FOURIER_W + ATTN_W + 2 * KV_W
AB_MIX_W = FOURIER_W + ATTN_W
Q_BLOCK = 128
ROPE_THETA = 10000.0
ROPE_HALF = HEAD_DIM // 2
ROPE_FREQS = ROPE_HALF // 2
M_HEADS = 4
M_W = D_MODEL
M_HEAD_DIM = M_W // M_HEADS
M_CONV = 3
M_CHUNK = 64
M_IN_W = 4 * M_W + 4 * M_HEADS

kernel_name = "hybrid_fnet_gqa_mlstm_macaron_encoder"


def rmsnorm(x, g):
    xf = x.astype(jnp.float32)
    y = xf * lax.rsqrt(jnp.mean(xf * xf, axis=-1, keepdims=True) + EPS)
    return (y * g.astype(jnp.float32)).astype(x.dtype)


def swiglu(x, w_in, w_out):
    gate, up = jnp.split(x @ w_in, 2, axis=-1)
    return (jax.nn.silu(gate) * up) @ w_out


def axial_rope_tables(n_tok):
    rows = n_tok // GRID_W
    row_id = jnp.repeat(jnp.arange(rows, dtype=jnp.float32), GRID_W)
    col_id = jnp.tile(jnp.arange(GRID_W, dtype=jnp.float32), rows)
    freqs = ROPE_THETA ** (-jnp.arange(ROPE_FREQS, dtype=jnp.float32) / ROPE_FREQS)
    ang_r = row_id[:, None] * freqs[None, :]
    ang_c = col_id[:, None] * freqs[None, :]
    return jnp.cos(ang_r), jnp.sin(ang_r), jnp.cos(ang_c), jnp.sin(ang_c)


def _rotate(xh, c, s):
    x1, x2 = jnp.split(xh, 2, axis=-1)
    c = c[None, :, None, :]
    s = s[None, :, None, :]
    return jnp.concatenate([x1 * c - x2 * s, x1 * s + x2 * c], axis=-1)


def apply_axial_rope(x, tables):
    cr, sr, cc, sc = tables
    xf = x.astype(jnp.float32)
    xr, xc = jnp.split(xf, 2, axis=-1)
    return jnp.concatenate([_rotate(xr, cr, sr), _rotate(xc, cc, sc)], axis=-1).astype(x.dtype)


def gqa_bidirectional(q, k, v):
    B, S = q.shape[0], q.shape[1]
    nblk = S // Q_BLOCK
    qg = q.reshape(B, nblk, Q_BLOCK, N_KV_HEADS, Q_PER_KV, HEAD_DIM).transpose(1, 0, 2, 3, 4, 5)
    scale = HEAD_DIM ** -0.5

    def block(qb):
        s = jnp.einsum('bqhgd,bkhd->bhgqk', qb, k, preferred_element_type=jnp.float32) * scale
        p = jax.nn.softmax(s, axis=-1)
        return jnp.einsum('bhgqk,bkhd->bqhgd', p.astype(v.dtype), v)

    o = lax.map(block, qg)
    return o.transpose(1, 0, 2, 3, 4, 5).reshape(B, S, ATTN_W)


def fourier_gqa_mixer(u, w_in, q_norm, k_norm, w_out):
    B, S, _ = u.shape
    z = u @ w_in
    f, q, k, v = jnp.split(z, [FOURIER_W, FOURIER_W + ATTN_W, FOURIER_W + ATTN_W + KV_W], axis=-1)
    fg = f.reshape(B, S, FOURIER_GROUPS, FOURIER_CH).astype(jnp.float32)
    f_mix = jnp.fft.fft2(fg, axes=(1, 3), norm='ortho').real.reshape(B, S, FOURIER_W).astype(u.dtype)
    q = rmsnorm(q.reshape(B, S, N_Q_HEADS, HEAD_DIM), q_norm)
    k = rmsnorm(k.reshape(B, S, N_KV_HEADS, HEAD_DIM), k_norm)
    v = v.reshape(B, S, N_KV_HEADS, HEAD_DIM)
    tables = axial_rope_tables(S)
    q = apply_axial_rope(q, tables)
    k = apply_axial_rope(k, tables)
    a = gqa_bidirectional(q, k, v)
    return jnp.concatenate([f_mix, a], axis=-1) @ w_out


def centred_conv(x, w):
    S = x.shape[1]
    pad = M_CONV // 2
    xp = jnp.pad(x, ((0, 0), (pad, pad), (0, 0)))
    return sum(xp[:, j:j + S] * w[j] for j in range(M_CONV))


def mlstm_chunkwise(q, k, v, ig, lf):
    B, H, S, d = q.shape
    nC = S // M_CHUNK

    def chunks(a):
        return jnp.moveaxis(a.reshape(a.shape[:2] + (nC, M_CHUNK) + a.shape[3:]), 2, 0)

    tril = jnp.tril(jnp.ones((M_CHUNK, M_CHUNK), dtype=bool))

    def step(carry, xs):
        C, n, m = carry
        qc, kc, vc, ic, fc = xs
        b = jnp.cumsum(fc, axis=-1)
        D = jnp.where(tril, b[..., :, None] - b[..., None, :] + ic[..., None, :], -jnp.inf)
        inter = b + m[..., None]
        mj = jnp.maximum(inter, jnp.max(D, axis=-1))
        w_inter = jnp.exp(inter - mj)
        P = jnp.exp(D - mj[..., None])
        sqk = jnp.einsum('bhjd,bhsd->bhjs', qc, kc) * P
        num = (w_inter[..., None] * jnp.einsum('bhed,bhjd->bhje', C, qc)
               + jnp.einsum('bhjs,bhse->bhje', sqk, vc))
        den = w_inter * jnp.einsum('bhd,bhjd->bhj', n, qc) + jnp.sum(sqk, axis=-1)
        h = num / jnp.maximum(jnp.abs(den), jnp.exp(-mj))[..., None]
        bL = b[..., -1]
        gs = bL[..., None] - b + ic
        m_new = jnp.maximum(bL + m, jnp.max(gs, axis=-1))
        decay = jnp.exp(bL + m - m_new)
        ws = jnp.exp(gs - m_new[..., None])
        C = decay[..., None, None] * C + jnp.einsum('bhse,bhsd->bhed', vc * ws[..., None], kc)
        n = decay[..., None] * n + jnp.einsum('bhs,bhsd->bhd', ws, kc)
        return (C, n, m_new), h

    init = (jnp.zeros((B, H, d, d), jnp.float32), jnp.zeros((B, H, d), jnp.float32),
            jnp.zeros((B, H), jnp.float32))
    _, hs = lax.scan(step, init, (chunks(q), chunks(k), chunks(v), chunks(ig), chunks(lf)))
    return jnp.moveaxis(hs, 0, 2).reshape(B, H, S, d)


def mlstm_mixer(u, w_in, gate_bias, conv_w, head_norm, w_out):
    B, S, _ = u.shape
    z = u @ w_in
    qk, v, o, g = jnp.split(z, [2 * M_W, 3 * M_W, 4 * M_W], axis=-1)
    qk = jax.nn.silu(centred_conv(qk, conv_w))
    q, k = jnp.split(qk, 2, axis=-1)
    g = g.astype(jnp.float32) + gate_bias.astype(jnp.float32)
    i_f, f_f, i_b, f_b = jnp.split(g, 4, axis=-1)

    def heads(a):
        return a.reshape(B, S, M_HEADS, M_HEAD_DIM).transpose(0, 2, 1, 3).astype(jnp.float32)

    qh = heads(q)
    kh = heads(k) * (M_HEAD_DIM ** -0.5)
    vh = heads(v)
    tg = lambda a: a.transpose(0, 2, 1)
    h_f = mlstm_chunkwise(qh, kh, vh, tg(i_f), jax.nn.log_sigmoid(tg(f_f)))
    flip = lambda a: jnp.flip(a, axis=2)
    h_b = flip(mlstm_chunkwise(flip(qh), flip(kh), flip(vh), flip(tg(i_b)),
                               flip(jax.nn.log_sigmoid(tg(f_b)))))
    h = (h_f + h_b).transpose(0, 2, 1, 3).astype(u.dtype)
    h = rmsnorm(h, head_norm.reshape(M_HEADS, M_HEAD_DIM)).reshape(B, S, M_W)
    return (jax.nn.sigmoid(o) * h) @ w_out


def trunk(x, ffn1_norm, ffn1_w_in, ffn1_w_out, mix_norm, ab_w_in, ab_q_norm, ab_k_norm, ab_w_out,
          c_w_in, c_gate_bias, c_conv, c_head_norm, c_w_out, ffn2_norm, ffn2_w_in, ffn2_w_out):
    for l in range(DEPTH):
        x = x + 0.5 * swiglu(rmsnorm(x, ffn1_norm[l]), ffn1_w_in[l], ffn1_w_out[l])
        u = rmsnorm(x, mix_norm[l])
        if l % 2 == 0:
            j = l // 2
            x = x + fourier_gqa_mixer(u, ab_w_in[j], ab_q_norm[j], ab_k_norm[j], ab_w_out[j])
        else:
            j = l // 2
            x = x + mlstm_mixer(u, c_w_in[j], c_gate_bias[j], c_conv[j], c_head_norm[j], c_w_out[j])
        x = x + 0.5 * swiglu(rmsnorm(x, ffn2_norm[l]), ffn2_w_in[l], ffn2_w_out[l])
    return x


def setup_inputs(seed: int = 0) -> dict:
    key = jax.random.key(seed)
    ks = jax.random.split(key, 24)
    f32 = jnp.float32
    nrm = lambda k, shape, scale: jax.random.normal(k, shape, f32) * scale
    gain = lambda k, shape: 1.0 + 0.02 * jax.random.normal(k, shape, f32)
    f_bias = jnp.linspace(3.0, 6.0, M_HEADS, dtype=f32)
    gb_noise = nrm(ks[22], (N_ODD, 4, M_HEADS), 0.1)
    gate_bias = (gb_noise + jnp.stack([jnp.zeros_like(f_bias), f_bias,
                                      jnp.zeros_like(f_bias), f_bias])[None]).reshape(N_ODD, 4 * M_HEADS)
    return {
        "x_prompt": jax.random.normal(ks[0], (BATCH, SEQ, D_MODEL), f32),
        "x_sample": jax.random.normal(ks[1], (DEC_BATCH, DEC_SEQ, D_MODEL), f32),
        "ffn1_norm": gain(ks[2], (DEPTH, D_MODEL)),
        "ffn1_w_in": nrm(ks[3], (DEPTH, D_MODEL, 2 * D_FF), D_MODEL ** -0.5),
        "ffn1_w_out": nrm(ks[4], (DEPTH, D_FF, D_MODEL), D_FF ** -0.5),
        "mix_norm": gain(ks[5], (DEPTH, D_MODEL)),
        "ab_w_in": nrm(ks[6], (N_EVEN, D_MODEL, AB_IN_W), D_MODEL ** -0.5),
        "ab_q_norm": gain(ks[7], (N_EVEN, HEAD_DIM)),
        "ab_k_norm": gain(ks[8], (N_EVEN, HEAD_DIM)),
        "ab_w_out": nrm(ks[9], (N_EVEN, AB_MIX_W, D_MODEL), AB_MIX_W ** -0.5),
        "c_w_in": nrm(ks[10], (N_ODD, D_MODEL, M_IN_W), D_MODEL ** -0.5),
        "c_gate_bias": gate_bias,
        "c_conv": nrm(ks[11], (N_ODD, M_CONV, 2 * M_W), M_CONV ** -0.5),
        "c_head_norm": gain(ks[12], (N_ODD, M_W)),
        "c_w_out": nrm(ks[13], (N_ODD, M_W, D_MODEL), M_W ** -0.5),
        "ffn2_norm": gain(ks[14], (DEPTH, D_MODEL)),
        "ffn2_w_in": nrm(ks[15], (DEPTH, D_MODEL, 2 * D_FF), D_MODEL ** -0.5),
        "ffn2_w_out": nrm(ks[16], (DEPTH, D_FF, D_MODEL), D_FF ** -0.5),
    }


def reference(x_prompt, x_sample, ffn1_norm, ffn1_w_in, ffn1_w_out, mix_norm, ab_w_in, ab_q_norm,
              ab_k_norm, ab_w_out, c_w_in, c_gate_bias, c_conv, c_head_norm, c_w_out,
              ffn2_norm, ffn2_w_in, ffn2_w_out):
    y_prompt = trunk(x_prompt, ffn1_norm, ffn1_w_in, ffn1_w_out, mix_norm, ab_w_in, ab_q_norm,
                     ab_k_norm, ab_w_out, c_w_in, c_gate_bias, c_conv, c_head_norm, c_w_out,
                     ffn2_norm, ffn2_w_in, ffn2_w_out)
    y_sample = trunk(x_sample, ffn1_norm, ffn1_w_in, ffn1_w_out, mix_norm, ab_w_in, ab_q_norm,
                     ab_k_norm, ab_w_out, c_w_in, c_gate_bias, c_conv, c_head_norm, c_w_out,
                     ffn2_norm, ffn2_w_in, ffn2_w_out)
    return (y_prompt, y_sample)
```

```python
import functools

import jax
import jax.numpy as jnp
from jax import lax
from jax.experimental import pallas as pl
from jax.experimental.pallas import tpu as pltpu

F32 = jnp.float32
BF16 = jnp.bfloat16

D_MODEL = 1024
D_FF = 2816
EPS = 1e-6
GRID_W = 64
FOURIER_W = 256
FOURIER_CH = 64
HEAD_DIM = 64
ATTN_W = 768
N_KV_HEADS = 4
Q_PER_KV = 3
KV_W = 256
ROPE_THETA = 10000.0
ROPE_FREQS = 16
M_HEADS = 4
M_W = 1024
M_HEAD_DIM = 256
N_GATES = 4 * M_HEADS

V7X_VMEM_BYTES = 64 * 1024 * 1024
LANES = 128
SUBLANES = 8

TOKEN_TILE = 512
ATTN_Q_TILE = 256
ATTN_KV_TILE = 512
DFT_TILE = 1024
MLSTM_CHUNK = 256
FFN_CHUNKS = 2
VMEM_LIMIT = V7X_VMEM_BYTES - 8 * 1024 * 1024


def _params(*semantics):
    return pltpu.CompilerParams(dimension_semantics=semantics, vmem_limit_bytes=VMEM_LIMIT)


def _resident(shape):
    zeros = (0,) * len(shape)
    return pl.BlockSpec(shape, lambda *_: zeros, pipeline_mode=pl.Buffered(1))


def _dot(a, b):
    return jnp.dot(a, b, preferred_element_type=F32)


def _dot_nt(a, b):
    return lax.dot_general(a, b, (((1,), (1,)), ((), ())), preferred_element_type=F32)


def _dot_tn(a, b):
    return lax.dot_general(a, b, (((0,), (0,)), ((), ())), preferred_element_type=F32)


def _rmsnorm_bf16(x, gain):
    ms = jnp.mean(x * x, axis=-1, keepdims=True)
    return (x * lax.rsqrt(ms + EPS) * gain).astype(BF16)


def _ffn_body(x_ref, g_ref, win_ref, wout_ref, o_ref):
    x = x_ref[...]
    u = _rmsnorm_bf16(x, g_ref[...])
    fc = D_FF // FFN_CHUNKS
    acc = None
    for j in range(FFN_CHUNKS):
        gate = _dot(u, win_ref[:, j * fc:(j + 1) * fc])
        up = _dot(u, win_ref[:, D_FF + j * fc:D_FF + (j + 1) * fc])
        h = (gate * jax.nn.sigmoid(gate) * up).astype(BF16)
        part = _dot(h, wout_ref[j * fc:(j + 1) * fc, :])
        acc = part if acc is None else acc + part
    o_ref[...] = x + 0.5 * acc


def _ffn(x, gain, w_in, w_out):
    B, S, D = x.shape
    T = B * S
    tm = min(TOKEN_TILE, T)
    out = pl.pallas_call(
        _ffn_body,
        grid=(T // tm,),
        in_specs=[
            pl.BlockSpec((tm, D), lambda i: (i, 0)),
            _resident((1, D)),
            _resident((D, 2 * D_FF)),
            _resident((D_FF, D)),
        ],
        out_specs=pl.BlockSpec((tm, D), lambda i: (i, 0)),
        out_shape=jax.ShapeDtypeStruct((T, D), F32),
        compiler_params=_params("parallel"),
        name="ffn",
    )(x.reshape(T, D), gain, w_in, w_out)
    return out.reshape(B, S, D)


def _rope(xn, cos, sin, low_half):
    partner = jnp.where(low_half, pltpu.roll(xn, LANES - ROPE_FREQS, 1), pltpu.roll(xn, ROPE_FREQS, 1))
    return xn * cos + partner * sin


def _ab_in_body(x_ref, g_ref, w_ref, qg_ref, kg_ref, bd_ref, cos_ref, sin_ref, dft_ref,
                q_ref, k_ref, v_ref, y_ref, z_ref):
    u = _rmsnorm_bf16(x_ref[...], g_ref[...])
    z = _dot(u, w_ref[...])
    f = z[:, :FOURIER_W].astype(BF16)
    yz = _dot(f, dft_ref[...])
    y_ref[...] = yz[:, :FOURIER_W].astype(BF16)
    z_ref[...] = yz[:, FOURIER_W:].astype(BF16)
    v_ref[...] = z[:, FOURIER_W + ATTN_W + KV_W:].astype(BF16)

    cos = cos_ref[...]
    sin = sin_ref[...]
    lane = lax.broadcasted_iota(jnp.int32, (1, LANES), 1)
    low_half = (lane % (2 * ROPE_FREQS)) < ROPE_FREQS
    bd = bd_ref[...]

    def norm_rope(xs, gain):
        ms = _dot((xs * xs).astype(BF16), bd)
        xn = xs * lax.rsqrt(ms + EPS) * gain
        halves = [_rope(xn[:, i * LANES:(i + 1) * LANES], cos, sin, low_half) for i in range(2)]
        return jnp.concatenate(halves, axis=-1).astype(BF16)

    for c in range(ATTN_W // KV_W):
        lo = FOURIER_W + c * KV_W
        q_ref[:, c * KV_W:(c + 1) * KV_W] = norm_rope(z[:, lo:lo + KV_W], qg_ref[:, c * KV_W:(c + 1) * KV_W])
    lo = FOURIER_W + ATTN_W
    k_ref[...] = norm_rope(z[:, lo:lo + KV_W], kg_ref[...])


def _ab_in(x, gain, w, qg, kg, bd, cos, sin, dft):
    B, S, D = x.shape
    tm = min(TOKEN_TILE, S)
    tok = lambda width: pl.BlockSpec((None, tm, width), lambda b, s: (b, s, 0))
    spec = lambda width: pl.BlockSpec((tm, width), lambda b, s: (s, b))
    return pl.pallas_call(
        _ab_in_body,
        grid=(B, S // tm),
        in_specs=[
            tok(D),
            _resident((1, D)),
            _resident(w.shape),
            _resident(qg.shape),
            _resident(kg.shape),
            _resident(bd.shape),
            pl.BlockSpec((tm, LANES), lambda b, s: (s, 0)),
            pl.BlockSpec((tm, LANES), lambda b, s: (s, 0)),
            _resident(dft.shape),
        ],
        out_specs=[tok(ATTN_W), tok(KV_W), tok(KV_W), spec(FOURIER_W), spec(FOURIER_W)],
        out_shape=[
            jax.ShapeDtypeStruct((B, S, ATTN_W), BF16),
            jax.ShapeDtypeStruct((B, S, KV_W), BF16),
            jax.ShapeDtypeStruct((B, S, KV_W), BF16),
            jax.ShapeDtypeStruct((S, B * FOURIER_W), BF16),
            jax.ShapeDtypeStruct((S, B * FOURIER_W), BF16),
        ],
        compiler_params=_params("parallel", "parallel"),
        name="ab_in",
    )(x, gain, w, qg, kg, bd, cos, sin, dft)


def _dft_body(c_ref, s_ref, y_ref, z_ref, o_ref, acc_ref, *, scale):
    kk = pl.program_id(2)
    part = _dot(c_ref[...], y_ref[...]) + _dot(s_ref[...], z_ref[...])

    @pl.when(kk == 0)
    def _():
        acc_ref[...] = part

    @pl.when(kk > 0)
    def _():
        acc_ref[...] += part

    @pl.when(kk == pl.num_programs(2) - 1)
    def _():
        o_ref[...] = (acc_ref[...] * scale).astype(BF16)


def _seq_dft(cmat, smat, y, z):
    S, N = y.shape
    tm = min(DFT_TILE, S)
    tn = min(DFT_TILE, N)
    tk = min(DFT_TILE, S)
    return pl.pallas_call(
        functools.partial(_dft_body, scale=float(S) ** -0.5),
        grid=(S // tm, N // tn, S // tk),
        in_specs=[
            pl.BlockSpec((tm, tk), lambda i, j, k: (i, k)),
            pl.BlockSpec((tm, tk), lambda i, j, k: (i, k)),
            pl.BlockSpec((tk, tn), lambda i, j, k: (k, j)),
            pl.BlockSpec((tk, tn), lambda i, j, k: (k, j)),
        ],
        out_specs=pl.BlockSpec((tm, tn), lambda i, j, k: (i, j)),
        out_shape=jax.ShapeDtypeStruct((S, N), BF16),
        scratch_shapes=[pltpu.VMEM((tm, tn), F32)],
        compiler_params=_params("parallel", "parallel", "arbitrary"),
        name="seq_dft",
    )(cmat, smat, y, z)


def _attn_body(q_ref, k_ref, v_ref, o_ref, *, tk):
    tq = q_ref.shape[0]
    n_kv = k_ref.shape[0] // tk
    rows = Q_PER_KV * tq
    for j in range(N_KV_HEADS):
        heads = [Q_PER_KV * j + g for g in range(Q_PER_KV)]
        qg = jnp.concatenate([q_ref[:, h * HEAD_DIM:(h + 1) * HEAD_DIM] for h in heads], axis=0)

        def step(i, carry, j=j, qg=qg):
            m, l, acc = carry
            start = pl.multiple_of(i * tk, tk)
            ks = k_ref[pl.ds(start, tk), j * HEAD_DIM:(j + 1) * HEAD_DIM]
            vs = v_ref[pl.ds(start, tk), j * HEAD_DIM:(j + 1) * HEAD_DIM]
            s = _dot_nt(qg, ks)
            m_new = jnp.maximum(m, jnp.max(s, axis=-1, keepdims=True))
            alpha = jnp.exp(m - m_new)
            p = jnp.exp(s - m_new)
            l = alpha * l + jnp.sum(p, axis=-1, keepdims=True)
            acc = alpha * acc + _dot(p.astype(BF16), vs)
            return m_new, l, acc

        init = (jnp.full((rows, 1), -jnp.inf, F32), jnp.zeros((rows, 1), F32),
                jnp.zeros((rows, HEAD_DIM), F32))
        _, l, acc = lax.fori_loop(0, n_kv, step, init)
        out = (acc / l).astype(BF16)
        for g, h in enumerate(heads):
            o_ref[:, h * HEAD_DIM:(h + 1) * HEAD_DIM] = out[g * tq:(g + 1) * tq]


def _attention(q, k, v):
    B, S, _ = q.shape
    tq = min(ATTN_Q_TILE, S)
    tk = min(ATTN_KV_TILE, S)
    return pl.pallas_call(
        functools.partial(_attn_body, tk=tk),
        grid=(B, S // tq),
        in_specs=[
            pl.BlockSpec((None, tq, ATTN_W), lambda b, i: (b, i, 0)),
            pl.BlockSpec((None, S, KV_W), lambda b, i: (b, 0, 0)),
            pl.BlockSpec((None, S, KV_W), lambda b, i: (b, 0, 0)),
        ],
        out_specs=pl.BlockSpec((None, tq, ATTN_W), lambda b, i: (b, i, 0)),
        out_shape=jax.ShapeDtypeStruct((B, S, ATTN_W), BF16),
        compiler_params=_params("parallel", "arbitrary"),
        name="gqa",
    )(q, k, v)


def _ab_out_body(x_ref, f_ref, a_ref, wf_ref, wa_ref, o_ref):
    o_ref[...] = x_ref[...] + _dot(f_ref[...], wf_ref[...]) + _dot(a_ref[...], wa_ref[...])


def _ab_out(x, fmix, attn, wf, wa):
    B, S, D = x.shape
    tm = min(TOKEN_TILE, S)
    tok = lambda width: pl.BlockSpec((None, tm, width), lambda b, s: (b, s, 0))
    return pl.pallas_call(
        _ab_out_body,
        grid=(B, S // tm),
        in_specs=[
            tok(D),
            pl.BlockSpec((tm, FOURIER_W), lambda b, s: (s, b)),
            tok(ATTN_W),
            _resident(wf.shape),
            _resident(wa.shape),
        ],
        out_specs=tok(D),
        out_shape=jax.ShapeDtypeStruct((B, S, D), F32),
        compiler_params=_params("parallel", "parallel"),
        name="ab_out",
    )(x, fmix, attn, wf, wa)


def _c_in_body(x_ref, xp_ref, xn_ref, g_ref, wqk_ref, wvo_ref, wg_ref, wgt_ref, bias_ref, biast_ref,
               cw_ref, kscale_ref, qk_ref, v_ref, o_ref, gate_ref, gatet_ref):
    tm = x_ref.shape[0]
    s = pl.program_id(1)
    last = pl.num_programs(1) - 1
    xe = jnp.concatenate([xp_ref[...], x_ref[...], xn_ref[...]], axis=0)
    ue = _rmsnorm_bf16(xe, g_ref[...])
    zqk = _dot(ue, wqk_ref[...])
    row = lax.broadcasted_iota(jnp.int32, (tm + 2 * SUBLANES, 1), 0)
    inside = jnp.logical_and(jnp.logical_or(row >= SUBLANES, s > 0),
                             jnp.logical_or(row < tm + SUBLANES, s < last))
    zqk = jnp.where(inside, zqk, 0.0)
    conv = (cw_ref[0:1, :] * zqk[SUBLANES - 1:SUBLANES - 1 + tm]
            + cw_ref[1:2, :] * zqk[SUBLANES:SUBLANES + tm]
            + cw_ref[2:3, :] * zqk[SUBLANES + 1:SUBLANES + 1 + tm])
    qk_ref[...] = (conv * jax.nn.sigmoid(conv) * kscale_ref[...]).astype(BF16)

    u = ue[SUBLANES:SUBLANES + tm]
    zvo = _dot(u, wvo_ref[...])
    v_ref[...] = zvo[:, :M_W].astype(BF16)
    o_ref[...] = zvo[:, M_W:].astype(BF16)
    gate_ref[...] = _dot(u, wg_ref[...]) + bias_ref[...]
    gatet_ref[...] = _dot_nt(wgt_ref[...], u) + biast_ref[...]


def _c_in(x, gain, wqk, wvo, wg, wgt, bias, biast, cw, kscale):
    B, S, D = x.shape
    tm = min(TOKEN_TILE, S)
    per = tm // SUBLANES
    n_halo = S // SUBLANES
    tok = lambda width: pl.BlockSpec((None, tm, width), lambda b, s: (b, s, 0))
    return pl.pallas_call(
        _c_in_body,
        grid=(B, S // tm),
        in_specs=[
            tok(D),
            pl.BlockSpec((None, SUBLANES, D), lambda b, s: (b, jnp.maximum(s * per - 1, 0), 0)),
            pl.BlockSpec((None, SUBLANES, D), lambda b, s: (b, jnp.minimum((s + 1) * per, n_halo - 1), 0)),
            _resident((1, D)),
            _resident(wqk.shape),
            _resident(wvo.shape),
            _resident(wg.shape),
            _resident(wgt.shape),
            _resident(bias.shape),
            _resident(biast.shape),
            _resident(cw.shape),
            _resident(kscale.shape),
        ],
        out_specs=[tok(2 * M_W), tok(M_W), tok(M_W), tok(LANES),
                   pl.BlockSpec((None, N_GATES, tm), lambda b, s: (b, 0, s))],
        out_shape=[
            jax.ShapeDtypeStruct((B, S, 2 * M_W), BF16),
            jax.ShapeDtypeStruct((B, S, M_W), BF16),
            jax.ShapeDtypeStruct((B, S, M_W), BF16),
            jax.ShapeDtypeStruct((B, S, LANES), F32),
            jax.ShapeDtypeStruct((B, N_GATES, S), F32),
        ],
        compiler_params=_params("parallel", "parallel"),
        name="c_in",
    )(x, x, x, gain, wqk, wvo, wg, wgt, bias, biast, cw, kscale)


def _split3(x):
    hi = x.astype(BF16)
    r1 = x - hi.astype(F32)
    mid = r1.astype(BF16)
    lo = (r1 - mid.astype(F32)).astype(BF16)
    return hi, mid, lo


def _log_sigmoid(x):
    return jnp.minimum(x, 0.0) - jnp.log1p(jnp.exp(-jnp.abs(x)))


def _mlstm_dir(qk_ref, v_ref, g_ref, gt_ref, h_ref, ct_ref, n_ref, m_ref, *, backward):
    L = qk_ref.shape[0]
    t_idx = lax.broadcasted_iota(jnp.int32, (L, L), 0)
    s_idx = lax.broadcasted_iota(jnp.int32, (L, L), 1)
    keep = (s_idx >= t_idx) if backward else (s_idx <= t_idx)
    tri = jnp.where(keep, 1.0, 0.0).astype(BF16)
    tri_t = jnp.where((t_idx >= s_idx) if backward else (t_idx <= s_idx), 1.0, 0.0).astype(BF16)
    end = 0 if backward else L - 1
    d = 1 if backward else 0

    gates = g_ref[...]
    gates_t = gt_ref[...]
    cum = sum(_dot(tri, piece) for piece in _split3(_log_sigmoid(gates)))
    cum_t = sum(_dot(piece, tri_t) for piece in _split3(_log_sigmoid(gates_t)))

    for h in range(M_HEADS):
        ci = 2 * d * M_HEADS + h
        cf = ci + M_HEADS
        r = d * M_HEADS + h
        q = qk_ref[:, h * M_HEAD_DIM:(h + 1) * M_HEAD_DIM]
        k = qk_ref[:, M_W + h * M_HEAD_DIM:M_W + (h + 1) * M_HEAD_DIM]
        v = v_ref[:, h * M_HEAD_DIM:(h + 1) * M_HEAD_DIM]
        b_col = cum[:, cf:cf + 1]
        b_row = cum_t[cf:cf + 1, :]
        i_col = gates[:, ci:ci + 1]
        i_row = gates_t[ci:ci + 1, :]
        m = m_ref[r:r + 1, 0:1]
        n = n_ref[r:r + 1, :]
        ct = ct_ref[r]

        dmat = jnp.where(keep, b_col - b_row + i_row, -jnp.inf)
        inter = b_col + m
        mj = jnp.maximum(inter, jnp.max(dmat, axis=-1, keepdims=True))
        w_inter = jnp.exp(inter - mj)
        sqk = _dot_nt(q, k) * jnp.exp(dmat - mj)
        num = w_inter * _dot(q, ct.astype(BF16)) + _dot(sqk.astype(BF16), v)
        qn = jnp.sum(q.astype(F32) * n, axis=-1, keepdims=True)
        den = w_inter * qn + jnp.sum(sqk, axis=-1, keepdims=True)
        hout = num / jnp.maximum(jnp.abs(den), jnp.exp(-mj))
        h_ref[:, h * M_HEAD_DIM:(h + 1) * M_HEAD_DIM] = hout.astype(BF16)

        b_end = cum[end:end + 1, cf:cf + 1]
        gs_col = b_end - b_col + i_col
        gs_row = b_end - b_row + i_row
        m_new = jnp.maximum(b_end + m, jnp.max(gs_row, axis=-1, keepdims=True))
        decay = jnp.exp(b_end + m - m_new)
        ws_col = jnp.exp(gs_col - m_new)
        ct_ref[r] = decay * ct + _dot_tn(k, (v.astype(F32) * ws_col).astype(BF16))
        n_ref[r:r + 1, :] = decay * n + jnp.sum(k.astype(F32) * ws_col, axis=0, keepdims=True)
        m_ref[r:r + 1, :] = jnp.broadcast_to(m_new, (1, LANES))


def _mlstm_body(qkf_ref, vf_ref, gf_ref, gtf_ref, qkb_ref, vb_ref, gb_ref, gtb_ref,
                hf_ref, hb_ref, ct_ref, n_ref, m_ref):
    @pl.when(pl.program_id(1) == 0)
    def _():
        ct_ref[...] = jnp.zeros_like(ct_ref)
        n_ref[...] = jnp.zeros_like(n_ref)
        m_ref[...] = jnp.zeros_like(m_ref)

    _mlstm_dir(qkf_ref, vf_ref, gf_ref, gtf_ref, hf_ref, ct_ref, n_ref, m_ref, backward=False)
    _mlstm_dir(qkb_ref, vb_ref, gb_ref, gtb_ref, hb_ref, ct_ref, n_ref, m_ref, backward=True)


def _mlstm(qk, v, gates, gates_t):
    B, S, _ = v.shape
    L = min(MLSTM_CHUNK, S)
    nc = S // L
    fwd = lambda width: pl.BlockSpec((None, L, width), lambda b, c: (b, c, 0))
    bwd = lambda width: pl.BlockSpec((None, L, width), lambda b, c: (b, nc - 1 - c, 0))
    return pl.pallas_call(
        _mlstm_body,
        grid=(B, nc),
        in_specs=[
            fwd(2 * M_W), fwd(M_W), fwd(LANES),
            pl.BlockSpec((None, N_GATES, L), lambda b, c: (b, 0, c)),
            bwd(2 * M_W), bwd(M_W), bwd(LANES),
            pl.BlockSpec((None, N_GATES, L), lambda b, c: (b, 0, nc - 1 - c)),
        ],
        out_specs=[fwd(M_W), bwd(M_W)],
        out_shape=[jax.ShapeDtypeStruct((B, S, M_W), BF16)] * 2,
        scratch_shapes=[
            pltpu.VMEM((2 * M_HEADS, M_HEAD_DIM, M_HEAD_DIM), F32),
            pltpu.VMEM((2 * M_HEADS, M_HEAD_DIM), F32),
            pltpu.VMEM((2 * M_HEADS, LANES), F32),
        ],
        compiler_params=_params("parallel", "arbitrary"),
        name="mlstm",
    )(qk, v, gates, gates_t, qk, v, gates, gates_t)


def _c_out_body(x_ref, hf_ref, hb_ref, o_ref, hg_ref, w_ref, out_ref):
    h = hf_ref[...].astype(F32) + hb_ref[...].astype(F32)
    normed = []
    for i in range(M_HEADS):
        hh = h[:, i * M_HEAD_DIM:(i + 1) * M_HEAD_DIM]
        ms = jnp.mean(hh * hh, axis=-1, keepdims=True)
        normed.append(hh * lax.rsqrt(ms + EPS))
    hn = jnp.concatenate(normed, axis=-1) * hg_ref[...]
    gated = (jax.nn.sigmoid(o_ref[...].astype(F32)) * hn).astype(BF16)
    out_ref[...] = x_ref[...] + _dot(gated, w_ref[...])


def _c_out(x, hf, hb, o, hg, w):
    B, S, D = x.shape
    tm = min(TOKEN_TILE, S)
    tok = lambda width: pl.BlockSpec((None, tm, width), lambda b, s: (b, s, 0))
    return pl.pallas_call(
        _c_out_body,
        grid=(B, S // tm),
        in_specs=[tok(D), tok(M_W), tok(M_W), tok(M_W), _resident((1, M_W)), _resident(w.shape)],
        out_specs=tok(D),
        out_shape=jax.ShapeDtypeStruct((B, S, D), F32),
        compiler_params=_params("parallel", "parallel"),
        name="c_out",
    )(x, hf, hb, o, hg, w)


def _rope_tables(S):
    t = jnp.arange(S, dtype=jnp.int32)
    row_id = (t // GRID_W).astype(F32)
    col_id = (t % GRID_W).astype(F32)
    freqs = ROPE_THETA ** (-jnp.arange(ROPE_FREQS, dtype=F32) / ROPE_FREQS)
    ang_r = row_id[:, None] * freqs[None, :]
    ang_c = col_id[:, None] * freqs[None, :]
    cos = jnp.concatenate([jnp.cos(ang_r)] * 2 + [jnp.cos(ang_c)] * 2, axis=-1)
    sin = jnp.concatenate([-jnp.sin(ang_r), jnp.sin(ang_r), -jnp.sin(ang_c), jnp.sin(ang_c)], axis=-1)
    return jnp.tile(cos, (1, LANES // HEAD_DIM)), jnp.tile(sin, (1, LANES // HEAD_DIM))


def _dft_matrices(n):
    j = jnp.arange(n, dtype=jnp.int32)
    phase = ((j[:, None] * j[None, :]) % n).astype(F32) * (2.0 * jnp.pi / n)
    return jnp.cos(phase), -jnp.sin(phase)


def _block_diag(block, copies):
    return jnp.kron(jnp.eye(copies, dtype=block.dtype), block)


def _prepare(ffn1_norm, ffn1_w_in, ffn1_w_out, mix_norm, ab_w_in, ab_q_norm, ab_k_norm, ab_w_out,
             c_w_in, c_gate_bias, c_conv, c_head_norm, c_w_out, ffn2_norm, ffn2_w_in, ffn2_w_out):
    groups = FOURIER_W // FOURIER_CH
    c64, s64 = _dft_matrices(FOURIER_CH)
    ch_scale = float(FOURIER_CH) ** -0.5
    p = {
        "ffn1": [(ffn1_norm[l][None], ffn1_w_in[l].astype(BF16), ffn1_w_out[l].astype(BF16)) for l in range(2)],
        "ffn2": [(ffn2_norm[l][None], ffn2_w_in[l].astype(BF16), ffn2_w_out[l].astype(BF16)) for l in range(2)],
        "mix_norm": [mix_norm[l][None] for l in range(2)],
        "ab_w_in": ab_w_in[0].astype(BF16),
        "qg": jnp.tile(ab_q_norm[0], ATTN_W // HEAD_DIM)[None] * (float(HEAD_DIM) ** -0.5),
        "kg": jnp.tile(ab_k_norm[0], KV_W // HEAD_DIM)[None],
        "bd": _block_diag(jnp.full((HEAD_DIM, HEAD_DIM), 1.0 / HEAD_DIM, F32), KV_W // HEAD_DIM).astype(BF16),
        "dft64": jnp.concatenate([_block_diag(c64 * ch_scale, groups),
                                  _block_diag(-s64 * ch_scale, groups)], axis=-1).astype(BF16),
        "ab_wf": ab_w_out[0][:FOURIER_W].astype(BF16),
        "ab_wa": ab_w_out[0][FOURIER_W:].astype(BF16),
        "c_wqk": c_w_in[0][:, :2 * M_W].astype(BF16),
        "c_wvo": c_w_in[0][:, 2 * M_W:4 * M_W].astype(BF16),
        "c_wg": jnp.pad(c_w_in[0][:, 4 * M_W:], ((0, 0), (0, LANES - N_GATES))).astype(BF16),
        "c_wgt": c_w_in[0][:, 4 * M_W:].T.astype(BF16),
        "c_bias": jnp.pad(c_gate_bias[0], (0, LANES - N_GATES))[None],
        "c_biast": c_gate_bias[0][:, None],
        "c_conv": c_conv[0],
        "c_kscale": jnp.concatenate([jnp.ones((M_W,), F32),
                                     jnp.full((M_W,), float(M_HEAD_DIM) ** -0.5, F32)])[None],
        "c_hg": c_head_norm[0][None],
        "c_wout": c_w_out[0].astype(BF16),
    }
    return p


def _trunk(x, p):
    B, S, _ = x.shape
    x = _ffn(x, *p["ffn1"][0])
    cos, sin = _rope_tables(S)
    q, k, v, y, z = _ab_in(x, p["mix_norm"][0], p["ab_w_in"], p["qg"], p["kg"], p["bd"], cos, sin, p["dft64"])
    cmat, smat = _dft_matrices(S)
    fmix = _seq_dft(cmat.astype(BF16), smat.astype(BF16), y, z)
    attn = _attention(q, k, v)
    x = _ab_out(x, fmix, attn, p["ab_wf"], p["ab_wa"])
    x = _ffn(x, *p["ffn2"][0])
    x = _ffn(x, *p["ffn1"][1])
    qk, v, o, gates, gates_t = _c_in(x, p["mix_norm"][1], p["c_wqk"], p["c_wvo"], p["c_wg"], p["c_wgt"],
                                     p["c_bias"], p["c_biast"], p["c_conv"], p["c_kscale"])
    hf, hb = _mlstm(qk, v, gates, gates_t)
    x = _c_out(x, hf, hb, o, p["c_hg"], p["c_wout"])
    x = _ffn(x, *p["ffn2"][1])
    return x


def kernel(x_prompt, x_sample, ffn1_norm, ffn1_w_in, ffn1_w_out, mix_norm, ab_w_in, ab_q_norm, ab_k_norm,
           ab_w_out, c_w_in, c_gate_bias, c_conv, c_head_norm, c_w_out, ffn2_norm, ffn2_w_in, ffn2_w_out):
    p = _prepare(ffn1_norm, ffn1_w_in, ffn1_w_out, mix_norm, ab_w_in, ab_q_norm, ab_k_norm, ab_w_out,
                 c_w_in, c_gate_bias, c_conv, c_head_norm, c_w_out, ffn2_norm, ffn2_w_in, ffn2_w_out)
    return (_trunk(x_prompt, p), _trunk(x_sample, p))
```

```python
import functools

import jax
import jax.numpy as jnp
from jax import lax
from jax.experimental import pallas as pl
from jax.experimental.pallas import tpu as pltpu

F32 = jnp.float32
BF16 = jnp.bfloat16

D_MODEL = 1024
D_FF = 2816
EPS = 1e-6
GRID_W = 64
FOURIER_W = 256
FOURIER_CH = 64
HEAD_DIM = 64
ATTN_W = 768
N_KV_HEADS = 4
Q_PER_KV = 3
KV_W = 256
PAD_KV_W = 4 * 128
ROPE_THETA = 10000.0
ROPE_FREQS = 16
M_HEADS = 4
M_W = 1024
M_HEAD_DIM = 256
N_GATES = 4 * M_HEADS

V7X_VMEM_BYTES = 64 * 1024 * 1024
LANES = 128
SUBLANES = 8

TOKEN_TILE = 512
ATTN_Q_TILE = 256
ATTN_KV_TILE = 1024
DFT_TILE = 1024
DFT_PHASE_ROWS = 64
MAX_SOFTMAX_SHIFT = 40.0
MLSTM_CHUNK = 256
FFN_CHUNKS = 1
VMEM_LIMIT = V7X_VMEM_BYTES - 8 * 1024 * 1024


def _params(*semantics):
    return pltpu.CompilerParams(dimension_semantics=semantics, vmem_limit_bytes=VMEM_LIMIT)


def _resident(shape):
    zeros = (0,) * len(shape)
    return pl.BlockSpec(shape, lambda *_: zeros, pipeline_mode=pl.Buffered(1))


def _dot(a, b):
    return jnp.dot(a, b, preferred_element_type=F32)


def _dot_nt(a, b):
    return lax.dot_general(a, b, (((1,), (1,)), ((), ())), preferred_element_type=F32)


def _dot_tn(a, b):
    return lax.dot_general(a, b, (((0,), (0,)), ((), ())), preferred_element_type=F32)


def _rmsnorm_bf16(x, gain):
    ms = jnp.mean(x * x, axis=-1, keepdims=True)
    return (x * lax.rsqrt(ms + EPS) * gain).astype(BF16)


def _ffn_body(x_ref, g_ref, win_ref, wout_ref, o_ref):
    x = x_ref[...]
    u = _rmsnorm_bf16(x, g_ref[...])
    fc = D_FF // FFN_CHUNKS
    acc = None
    for j in range(FFN_CHUNKS):
        gate = _dot(u, win_ref[:, j * fc:(j + 1) * fc])
        up = _dot(u, win_ref[:, D_FF + j * fc:D_FF + (j + 1) * fc])
        h = (gate * jax.nn.sigmoid(gate) * up).astype(BF16)
        part = _dot(h, wout_ref[j * fc:(j + 1) * fc, :])
        acc = part if acc is None else acc + part
    o_ref[...] = x + 0.5 * acc


def _ffn(x, gain, w_in, w_out):
    B, S, D = x.shape
    T = B * S
    tm = min(TOKEN_TILE, T)
    out = pl.pallas_call(
        _ffn_body,
        grid=(T // tm,),
        in_specs=[
            pl.BlockSpec((tm, D), lambda i: (i, 0)),
            _resident((1, D)),
            _resident((D, 2 * D_FF)),
            _resident((D_FF, D)),
        ],
        out_specs=pl.BlockSpec((tm, D), lambda i: (i, 0)),
        out_shape=jax.ShapeDtypeStruct((T, D), F32),
        compiler_params=_params("parallel"),
        name="ffn",
    )(x.reshape(T, D), gain, w_in, w_out)
    return out.reshape(B, S, D)


def _rope(xn, cos, sin, low_half):
    partner = jnp.where(low_half, pltpu.roll(xn, LANES - ROPE_FREQS, 1), pltpu.roll(xn, ROPE_FREQS, 1))
    return xn * cos + partner * sin


def _ab_in_body(x_ref, g_ref, w_ref, qg_ref, kg_ref, bd_ref, cos_ref, sin_ref, dft_ref, qpad_ref, kvpad_ref,
                q_ref, k_ref, v_ref, y_ref, z_ref):
    u = _rmsnorm_bf16(x_ref[...], g_ref[...])
    z = _dot(u, w_ref[...])
    f = z[:, :FOURIER_W].astype(BF16)
    yz = _dot(f, dft_ref[...])
    y_ref[...] = yz[:, :FOURIER_W].astype(BF16)
    z_ref[...] = yz[:, FOURIER_W:].astype(BF16)

    cos = cos_ref[...]
    sin = sin_ref[...]
    lane = lax.broadcasted_iota(jnp.int32, (1, LANES), 1)
    low_half = (lane % (2 * ROPE_FREQS)) < ROPE_FREQS
    is_head = lane < HEAD_DIM
    bd = bd_ref[...]

    def norm_rope(xs, gain):
        ms = _dot((xs * xs).astype(BF16), bd)
        xn = xs * lax.rsqrt(ms + EPS) * gain
        return [_rope(xn[:, i * LANES:(i + 1) * LANES], cos, sin, low_half) for i in range(2)]

    def pad_heads(pairs, pad):
        blocks = []
        for pair in pairs:
            blocks.append(jnp.where(is_head, pair, pad).astype(BF16))
            blocks.append(jnp.where(is_head, pltpu.roll(pair, HEAD_DIM, 1), pad).astype(BF16))
        return blocks

    n_qt, _, tq, _ = q_ref.shape
    per_group = KV_W // HEAD_DIM
    for c in range(ATTN_W // KV_W):
        lo = FOURIER_W + c * KV_W
        blocks = pad_heads(norm_rope(z[:, lo:lo + KV_W], qg_ref[:, c * KV_W:(c + 1) * KV_W]), qpad_ref[...])
        for i, blk in enumerate(blocks):
            for t in range(n_qt):
                q_ref[t, per_group * c + i] = blk[t * tq:(t + 1) * tq]
    lo = FOURIER_W + ATTN_W
    k_ref[...] = jnp.concatenate(pad_heads(norm_rope(z[:, lo:lo + KV_W], kg_ref[...]), kvpad_ref[...]), axis=-1)
    zv = z[:, lo + KV_W:]
    v_ref[...] = jnp.concatenate(pad_heads([zv[:, :LANES], zv[:, LANES:]], kvpad_ref[...]), axis=-1)


def _ab_in(x, gain, w, qg, kg, bd, cos, sin, dft, qpad, kvpad):
    B, S, D = x.shape
    tm = min(TOKEN_TILE, S)
    tq = min(ATTN_Q_TILE, S)
    n_heads = ATTN_W // HEAD_DIM
    assert tm % tq == 0
    tok = lambda width: pl.BlockSpec((None, tm, width), lambda b, s: (b, s, 0))
    spec = lambda width: pl.BlockSpec((tm, width), lambda b, s: (s, b))
    q_spec = pl.BlockSpec((None, tm // tq, n_heads, tq, LANES), lambda b, s: (b, s, 0, 0, 0))
    return pl.pallas_call(
        _ab_in_body,
        grid=(B, S // tm),
        in_specs=[
            tok(D),
            _resident((1, D)),
            _resident(w.shape),
            _resident(qg.shape),
            _resident(kg.shape),
            _resident(bd.shape),
            pl.BlockSpec((tm, LANES), lambda b, s: (s, 0)),
            pl.BlockSpec((tm, LANES), lambda b, s: (s, 0)),
            _resident(dft.shape),
            _resident(qpad.shape),
            _resident(kvpad.shape),
        ],
        out_specs=[q_spec, tok(PAD_KV_W), tok(PAD_KV_W), spec(FOURIER_W), spec(FOURIER_W)],
        out_shape=[
            jax.ShapeDtypeStruct((B, S // tq, n_heads, tq, LANES), BF16),
            jax.ShapeDtypeStruct((B, S, PAD_KV_W), BF16),
            jax.ShapeDtypeStruct((B, S, PAD_KV_W), BF16),
            jax.ShapeDtypeStruct((S, B * FOURIER_W), BF16),
            jax.ShapeDtypeStruct((S, B * FOURIER_W), BF16),
        ],
        compiler_params=_params("parallel", "parallel"),
        name="ab_in",
    )(x, gain, w, qg, kg, bd, cos, sin, dft, qpad, kvpad)


def _dft_body(c_ref, s_ref, y_ref, z_ref, o_ref, acc_ref, *, scale):
    kk = pl.program_id(2)
    part = _dot(c_ref[...], y_ref[...]) + _dot(s_ref[...], z_ref[...])

    @pl.when(kk == 0)
    def _():
        acc_ref[...] = part

    @pl.when(kk > 0)
    def _():
        acc_ref[...] += part

    @pl.when(kk == pl.num_programs(2) - 1)
    def _():
        o_ref[...] = (acc_ref[...] * scale).astype(BF16)


def _seq_dft(cmat, smat, y, z):
    S, N = y.shape
    tm = min(DFT_TILE, S)
    tn = min(DFT_TILE, N)
    tk = min(DFT_TILE, S)
    return pl.pallas_call(
        functools.partial(_dft_body, scale=float(S) ** -0.5),
        grid=(S // tm, N // tn, S // tk),
        in_specs=[
            pl.BlockSpec((tm, tk), lambda i, j, k: (i, k)),
            pl.BlockSpec((tm, tk), lambda i, j, k: (i, k)),
            pl.BlockSpec((tk, tn), lambda i, j, k: (k, j)),
            pl.BlockSpec((tk, tn), lambda i, j, k: (k, j)),
        ],
        out_specs=pl.BlockSpec((tm, tn), lambda i, j, k: (i, j)),
        out_shape=jax.ShapeDtypeStruct((S, N), BF16),
        scratch_shapes=[pltpu.VMEM((tm, tn), F32)],
        compiler_params=_params("parallel", "parallel", "arbitrary"),
        name="seq_dft",
    )(cmat, smat, y, z)


def _attn_body(q_ref, k_ref, v_ref, o_ref, *, tk, online):
    tq = q_ref.shape[1]
    n_kv = k_ref.shape[0] // tk
    rows = Q_PER_KV * tq
    lane = lax.broadcasted_iota(jnp.int32, (1, LANES), 1)

    def step(i, carry):
        start = pl.multiple_of(i * tk, tk)
        out = []
        for j in range(N_KV_HEADS):
            qg = q_ref[Q_PER_KV * j:Q_PER_KV * (j + 1)].reshape(rows, LANES)
            ks = k_ref[pl.ds(start, tk), j * LANES:(j + 1) * LANES]
            vs = v_ref[pl.ds(start, tk), j * LANES:(j + 1) * LANES]
            if not online:
                out.append(carry[j] + _dot(jnp.exp(_dot_nt(qg, ks)).astype(BF16), vs))
                continue
            s = _dot_nt(jnp.where(lane == HEAD_DIM, jnp.zeros_like(qg), qg), ks)
            m, acc = carry[j]
            m_new = jnp.maximum(m, jnp.max(s, axis=-1, keepdims=True))
            p = jnp.exp(s - m_new).astype(BF16)
            out.append((m_new, jnp.exp(m - m_new) * acc + _dot(p, vs)))
        return tuple(out)

    acc0 = jnp.zeros((rows, LANES), F32)
    init = (jnp.full((rows, 1), -jnp.inf, F32), acc0) if online else acc0
    final = lax.fori_loop(0, n_kv, step, (init,) * N_KV_HEADS)
    for j in range(N_KV_HEADS):
        acc = final[j][1] if online else final[j]
        out = (acc[:, :HEAD_DIM] / acc[:, HEAD_DIM:HEAD_DIM + 1]).astype(BF16)
        for g in range(Q_PER_KV):
            h = Q_PER_KV * j + g
            o_ref[:, h * HEAD_DIM:(h + 1) * HEAD_DIM] = out[g * tq:(g + 1) * tq]


def _attention(q, k, v, online):
    B, n_qt, n_heads, tq, _ = q.shape
    S = k.shape[1]
    tk = min(ATTN_KV_TILE, S)
    return pl.pallas_call(
        functools.partial(_attn_body, tk=tk, online=online),
        grid=(B, n_qt),
        in_specs=[
            pl.BlockSpec((None, None, n_heads, tq, LANES), lambda b, i: (b, i, 0, 0, 0)),
            pl.BlockSpec((None, S, PAD_KV_W), lambda b, i: (b, 0, 0)),
            pl.BlockSpec((None, S, PAD_KV_W), lambda b, i: (b, 0, 0)),
        ],
        out_specs=pl.BlockSpec((None, tq, ATTN_W), lambda b, i: (b, i, 0)),
        out_shape=jax.ShapeDtypeStruct((B, S, ATTN_W), BF16),
        compiler_params=_params("parallel", "arbitrary"),
        name="gqa_online" if online else "gqa_shifted",
    )(q, k, v)


def _ab_out_body(x_ref, f_ref, a_ref, wf_ref, wa_ref, o_ref):
    o_ref[...] = x_ref[...] + _dot(f_ref[...], wf_ref[...]) + _dot(a_ref[...], wa_ref[...])


def _ab_out(x, fmix, attn, wf, wa):
    B, S, D = x.shape
    tm = min(TOKEN_TILE, S)
    tok = lambda width: pl.BlockSpec((None, tm, width), lambda b, s: (b, s, 0))
    return pl.pallas_call(
        _ab_out_body,
        grid=(B, S // tm),
        in_specs=[
            tok(D),
            pl.BlockSpec((tm, FOURIER_W), lambda b, s: (s, b)),
            tok(ATTN_W),
            _resident(wf.shape),
            _resident(wa.shape),
        ],
        out_specs=tok(D),
        out_shape=jax.ShapeDtypeStruct((B, S, D), F32),
        compiler_params=_params("parallel", "parallel"),
        name="ab_out",
    )(x, fmix, attn, wf, wa)


def _c_in_body(x_ref, xp_ref, xn_ref, g_ref, wqk_ref, wvo_ref, wg_ref, wgt_ref, bias_ref, biast_ref,
               cw_ref, qk_ref, v_ref, o_ref, gate_ref, gatet_ref):
    tm = x_ref.shape[0]
    rows = tm + 2 * SUBLANES
    s = pl.program_id(1)
    xe = jnp.concatenate([xp_ref[...], x_ref[...], xn_ref[...]], axis=0)
    ue = _rmsnorm_bf16(xe, g_ref[...])
    z = _dot(ue, wqk_ref[...])
    row = lax.broadcasted_iota(jnp.int32, (rows, 1), 0)
    outside = jnp.logical_or(jnp.logical_and(row < SUBLANES, s == 0),
                             jnp.logical_and(row >= tm + SUBLANES, s == pl.num_programs(1) - 1))
    z = jnp.where(outside, 0.0, z)
    mid = slice(SUBLANES, SUBLANES + tm)
    conv = (cw_ref[0:1, :] * pltpu.roll(z, 1, 0)[mid]
            + cw_ref[1:2, :] * z[mid]
            + cw_ref[2:3, :] * pltpu.roll(z, rows - 1, 0)[mid])
    act = conv * jax.nn.sigmoid(conv)
    qk_ref[:, :M_W] = act[:, :M_W].astype(BF16)
    qk_ref[:, M_W:] = (act[:, M_W:] * (float(M_HEAD_DIM) ** -0.5)).astype(BF16)

    u = ue[SUBLANES:SUBLANES + tm]
    zvo = _dot(u, wvo_ref[...])
    v_ref[...] = zvo[:, :M_W].astype(BF16)
    o_ref[...] = zvo[:, M_W:].astype(BF16)
    gate_ref[...] = _dot(u, wg_ref[...]) + bias_ref[...]
    gatet_ref[...] = _dot_nt(wgt_ref[...], u) + biast_ref[...]


def _c_in(x, gain, wqk, wvo, wg, wgt, bias, biast, cw):
    B, S, D = x.shape
    tm = min(TOKEN_TILE, S)
    per = tm // SUBLANES
    n_halo = S // SUBLANES
    tok = lambda width: pl.BlockSpec((None, tm, width), lambda b, s: (b, s, 0))
    return pl.pallas_call(
        _c_in_body,
        grid=(B, S // tm),
        in_specs=[
            tok(D),
            pl.BlockSpec((None, SUBLANES, D), lambda b, s: (b, jnp.maximum(s * per - 1, 0), 0)),
            pl.BlockSpec((None, SUBLANES, D), lambda b, s: (b, jnp.minimum((s + 1) * per, n_halo - 1), 0)),
            _resident((1, D)),
            _resident(wqk.shape),
            _resident(wvo.shape),
            _resident(wg.shape),
            _resident(wgt.shape),
            _resident(bias.shape),
            _resident(biast.shape),
            _resident(cw.shape),
        ],
        out_specs=[tok(2 * M_W), tok(M_W), tok(M_W), tok(LANES),
                   pl.BlockSpec((None, N_GATES, tm), lambda b, s: (b, 0, s))],
        out_shape=[
            jax.ShapeDtypeStruct((B, S, 2 * M_W), BF16),
            jax.ShapeDtypeStruct((B, S, M_W), BF16),
            jax.ShapeDtypeStruct((B, S, M_W), BF16),
            jax.ShapeDtypeStruct((B, S, LANES), F32),
            jax.ShapeDtypeStruct((B, N_GATES, S), F32),
        ],
        compiler_params=_params("parallel", "parallel"),
        name="c_in",
    )(x, x, x, gain, wqk, wvo, wg, wgt, bias, biast, cw)


def _split3(x):
    hi = x.astype(BF16)
    r1 = x - hi.astype(F32)
    mid = r1.astype(BF16)
    lo = (r1 - mid.astype(F32)).astype(BF16)
    return hi, mid, lo


def _log_sigmoid(x):
    return jnp.minimum(x, 0.0) - jnp.log1p(jnp.exp(-jnp.abs(x)))


def _mlstm_dir(qk_ref, v_ref, g_ref, gt_ref, h_ref, ct_ref, n_ref, m_ref, *, backward):
    L = qk_ref.shape[0]
    t_idx = lax.broadcasted_iota(jnp.int32, (L, L), 0)
    s_idx = lax.broadcasted_iota(jnp.int32, (L, L), 1)
    keep = (s_idx >= t_idx) if backward else (s_idx <= t_idx)
    tri = jnp.where(keep, 1.0, 0.0).astype(BF16)
    tri_t = jnp.where((t_idx >= s_idx) if backward else (t_idx <= s_idx), 1.0, 0.0).astype(BF16)
    end = 0 if backward else L - 1
    d = 1 if backward else 0

    gates = g_ref[...]
    gates_t = gt_ref[...]
    cum = sum(_dot(tri, piece) for piece in _split3(_log_sigmoid(gates)))
    cum_t = sum(_dot(piece, tri_t) for piece in _split3(_log_sigmoid(gates_t)))

    for h in range(M_HEADS):
        ci = 2 * d * M_HEADS + h
        cf = ci + M_HEADS
        r = d * M_HEADS + h
        q = qk_ref[:, h * M_HEAD_DIM:(h + 1) * M_HEAD_DIM]
        k = qk_ref[:, M_W + h * M_HEAD_DIM:M_W + (h + 1) * M_HEAD_DIM]
        v = v_ref[:, h * M_HEAD_DIM:(h + 1) * M_HEAD_DIM]
        b_col = cum[:, cf:cf + 1]
        b_row = cum_t[cf:cf + 1, :]
        i_col = gates[:, ci:ci + 1]
        i_row = gates_t[ci:ci + 1, :]
        m = m_ref[r:r + 1, 0:1]
        n = n_ref[r:r + 1, :]
        ct = ct_ref[r]

        dmat = jnp.where(keep, b_col - b_row + i_row, -jnp.inf)
        inter = b_col + m
        mj = jnp.maximum(inter, jnp.max(dmat, axis=-1, keepdims=True))
        w_inter = jnp.exp(inter - mj)
        sqk = _dot_nt(q, k) * jnp.exp(dmat - mj)
        num = w_inter * _dot(q, ct.astype(BF16)) + _dot(sqk.astype(BF16), v)
        qn = jnp.sum(q.astype(F32) * n, axis=-1, keepdims=True)
        den = w_inter * qn + jnp.sum(sqk, axis=-1, keepdims=True)
        hout = num / jnp.maximum(jnp.abs(den), jnp.exp(-mj))
        h_ref[:, h * M_HEAD_DIM:(h + 1) * M_HEAD_DIM] = hout.astype(BF16)

        b_end = cum[end:end + 1, cf:cf + 1]
        gs_col = b_end - b_col + i_col
        gs_row = b_end - b_row + i_row
        m_new = jnp.maximum(b_end + m, jnp.max(gs_row, axis=-1, keepdims=True))
        decay = jnp.exp(b_end + m - m_new)
        ws_col = jnp.exp(gs_col - m_new)
        ct_ref[r] = decay * ct + _dot_tn(k, (v.astype(F32) * ws_col).astype(BF16))
        n_ref[r:r + 1, :] = decay * n + jnp.sum(k.astype(F32) * ws_col, axis=0, keepdims=True)
        m_ref[r:r + 1, :] = jnp.broadcast_to(m_new, (1, LANES))


def _mlstm_body(qkf_ref, vf_ref, gf_ref, gtf_ref, qkb_ref, vb_ref, gb_ref, gtb_ref,
                hf_ref, hb_ref, ct_ref, n_ref, m_ref):
    @pl.when(pl.program_id(1) == 0)
    def _():
        ct_ref[...] = jnp.zeros_like(ct_ref)
        n_ref[...] = jnp.zeros_like(n_ref)
        m_ref[...] = jnp.zeros_like(m_ref)

    _mlstm_dir(qkf_ref, vf_ref, gf_ref, gtf_ref, hf_ref, ct_ref, n_ref, m_ref, backward=False)
    _mlstm_dir(qkb_ref, vb_ref, gb_ref, gtb_ref, hb_ref, ct_ref, n_ref, m_ref, backward=True)


def _mlstm(qk, v, gates, gates_t):
    B, S, _ = v.shape
    L = min(MLSTM_CHUNK, S)
    nc = S // L
    fwd = lambda width: pl.BlockSpec((None, L, width), lambda b, c: (b, c, 0))
    bwd = lambda width: pl.BlockSpec((None, L, width), lambda b, c: (b, nc - 1 - c, 0))
    return pl.pallas_call(
        _mlstm_body,
        grid=(B, nc),
        in_specs=[
            fwd(2 * M_W), fwd(M_W), fwd(LANES),
            pl.BlockSpec((None, N_GATES, L), lambda b, c: (b, 0, c)),
            bwd(2 * M_W), bwd(M_W), bwd(LANES),
            pl.BlockSpec((None, N_GATES, L), lambda b, c: (b, 0, nc - 1 - c)),
        ],
        out_specs=[fwd(M_W), bwd(M_W)],
        out_shape=[jax.ShapeDtypeStruct((B, S, M_W), BF16)] * 2,
        scratch_shapes=[
            pltpu.VMEM((2 * M_HEADS, M_HEAD_DIM, M_HEAD_DIM), F32),
            pltpu.VMEM((2 * M_HEADS, M_HEAD_DIM), F32),
            pltpu.VMEM((2 * M_HEADS, LANES), F32),
        ],
        compiler_params=_params("parallel", "arbitrary"),
        name="mlstm",
    )(qk, v, gates, gates_t, qk, v, gates, gates_t)


def _c_out_body(x_ref, hf_ref, hb_ref, o_ref, hg_ref, w_ref, out_ref):
    h = hf_ref[...].astype(F32) + hb_ref[...].astype(F32)
    normed = []
    for i in range(M_HEADS):
        hh = h[:, i * M_HEAD_DIM:(i + 1) * M_HEAD_DIM]
        ms = jnp.mean(hh * hh, axis=-1, keepdims=True)
        normed.append(hh * lax.rsqrt(ms + EPS))
    hn = jnp.concatenate(normed, axis=-1) * hg_ref[...]
    gated = (jax.nn.sigmoid(o_ref[...].astype(F32)) * hn).astype(BF16)
    out_ref[...] = x_ref[...] + _dot(gated, w_ref[...])


def _c_out(x, hf, hb, o, hg, w):
    B, S, D = x.shape
    tm = min(TOKEN_TILE, S)
    tok = lambda width: pl.BlockSpec((None, tm, width), lambda b, s: (b, s, 0))
    return pl.pallas_call(
        _c_out_body,
        grid=(B, S // tm),
        in_specs=[tok(D), tok(M_W), tok(M_W), tok(M_W), _resident((1, M_W)), _resident(w.shape)],
        out_specs=tok(D),
        out_shape=jax.ShapeDtypeStruct((B, S, D), F32),
        compiler_params=_params("parallel", "parallel"),
        name="c_out",
    )(x, hf, hb, o, hg, w)


def _rope_tables(S):
    t = jnp.arange(S, dtype=jnp.int32)
    row_id = (t // GRID_W).astype(F32)
    col_id = (t % GRID_W).astype(F32)
    freqs = ROPE_THETA ** (-jnp.arange(ROPE_FREQS, dtype=F32) / ROPE_FREQS)
    ang_r = row_id[:, None] * freqs[None, :]
    ang_c = col_id[:, None] * freqs[None, :]
    cos = jnp.concatenate([jnp.cos(ang_r)] * 2 + [jnp.cos(ang_c)] * 2, axis=-1)
    sin = jnp.concatenate([-jnp.sin(ang_r), jnp.sin(ang_r), -jnp.sin(ang_c), jnp.sin(ang_c)], axis=-1)
    return jnp.tile(cos, (1, LANES // HEAD_DIM)), jnp.tile(sin, (1, LANES // HEAD_DIM))


def _dft_matrices(n):
    j = jnp.arange(n, dtype=jnp.int32)
    phase = ((j[:, None] * j[None, :]) % n).astype(F32) * (2.0 * jnp.pi / n)
    return jnp.cos(phase), -jnp.sin(phase)


def _seq_dft_matrices(n):
    rows = min(DFT_PHASE_ROWS, n)
    t = jnp.arange(n, dtype=jnp.int32)
    unit = 2.0 * jnp.pi / n
    pa = ((jnp.arange(0, n, rows, dtype=jnp.int32)[:, None] * t[None, :]) % n).astype(F32) * unit
    pb = ((jnp.arange(rows, dtype=jnp.int32)[:, None] * t[None, :]) % n).astype(F32) * unit
    ca, sa = jnp.cos(pa)[:, None, :], jnp.sin(pa)[:, None, :]
    cb, sb = jnp.cos(pb)[None], jnp.sin(pb)[None]
    cmat = (ca * cb - sa * sb).reshape(n, n).astype(BF16)
    smat = (-(sa * cb + ca * sb)).reshape(n, n).astype(BF16)
    return cmat, smat


def _block_diag(block, copies):
    return jnp.kron(jnp.eye(copies, dtype=block.dtype), block)


def _prepare(ffn1_norm, ffn1_w_in, ffn1_w_out, mix_norm, ab_w_in, ab_q_norm, ab_k_norm, ab_w_out,
             c_w_in, c_gate_bias, c_conv, c_head_norm, c_w_out, ffn2_norm, ffn2_w_in, ffn2_w_out):
    groups = FOURIER_W // FOURIER_CH
    c64, s64 = _dft_matrices(FOURIER_CH)
    ch_scale = float(FOURIER_CH) ** -0.5
    shift = 1.01 * float(HEAD_DIM) ** 0.5 * jnp.max(jnp.abs(ab_q_norm[0])) * jnp.max(jnp.abs(ab_k_norm[0]))
    p = {
        "ffn1": [(ffn1_norm[l][None], ffn1_w_in[l].astype(BF16), ffn1_w_out[l].astype(BF16)) for l in range(2)],
        "ffn2": [(ffn2_norm[l][None], ffn2_w_in[l].astype(BF16), ffn2_w_out[l].astype(BF16)) for l in range(2)],
        "mix_norm": [mix_norm[l][None] for l in range(2)],
        "ab_w_in": ab_w_in[0].astype(BF16),
        "qg": jnp.tile(ab_q_norm[0], ATTN_W // HEAD_DIM)[None] * (float(HEAD_DIM) ** -0.5),
        "kg": jnp.tile(ab_k_norm[0], KV_W // HEAD_DIM)[None],
        "bd": _block_diag(jnp.full((HEAD_DIM, HEAD_DIM), 1.0 / HEAD_DIM, F32), KV_W // HEAD_DIM).astype(BF16),
        "dft64": jnp.concatenate([_block_diag(c64 * ch_scale, groups),
                                  _block_diag(-s64 * ch_scale, groups)], axis=-1).astype(BF16),
        "ab_wf": ab_w_out[0][:FOURIER_W].astype(BF16),
        "ab_wa": ab_w_out[0][FOURIER_W:].astype(BF16),
        "c_wqk": c_w_in[0][:, :2 * M_W].astype(BF16),
        "c_wvo": c_w_in[0][:, 2 * M_W:4 * M_W].astype(BF16),
        "c_wg": jnp.pad(c_w_in[0][:, 4 * M_W:], ((0, 0), (0, LANES - N_GATES))).astype(BF16),
        "c_wgt": c_w_in[0][:, 4 * M_W:].T.astype(BF16),
        "c_bias": jnp.pad(c_gate_bias[0], (0, LANES - N_GATES))[None],
        "c_biast": c_gate_bias[0][:, None],
        "c_conv": c_conv[0],
        "c_hg": c_head_norm[0][None],
        "shift": shift,
        "qpad": jnp.zeros((1, LANES), F32).at[0, HEAD_DIM].set(-shift),
        "kvpad": jnp.zeros((1, LANES), F32).at[0, HEAD_DIM].set(1.0),
        "c_wout": c_w_out[0].astype(BF16),
    }
    return p


def _trunk(x, p):
    B, S, _ = x.shape
    x = _ffn(x, *p["ffn1"][0])
    cos, sin = _rope_tables(S)
    q, k, v, y, z = _ab_in(x, p["mix_norm"][0], p["ab_w_in"], p["qg"], p["kg"], p["bd"], cos, sin, p["dft64"],
                           p["qpad"], p["kvpad"])
    cmat, smat = _seq_dft_matrices(S)
    fmix = _seq_dft(cmat, smat, y, z)
    attn = lax.cond(p["shift"] <= MAX_SOFTMAX_SHIFT,
                    functools.partial(_attention, online=False),
                    functools.partial(_attention, online=True), q, k, v)
    x = _ab_out(x, fmix, attn, p["ab_wf"], p["ab_wa"])
    x = _ffn(x, *p["ffn2"][0])
    x = _ffn(x, *p["ffn1"][1])
    qk, v, o, gates, gates_t = _c_in(x, p["mix_norm"][1], p["c_wqk"], p["c_wvo"], p["c_wg"], p["c_wgt"],
                                     p["c_bias"], p["c_biast"], p["c_conv"])
    hf, hb = _mlstm(qk, v, gates, gates_t)
    x = _c_out(x, hf, hb, o, p["c_hg"], p["c_wout"])
    x = _ffn(x, *p["ffn2"][1])
    return x


def kernel(x_prompt, x_sample, ffn1_norm, ffn1_w_in, ffn1_w_out, mix_norm, ab_w_in, ab_q_norm, ab_k_norm,
           ab_w_out, c_w_in, c_gate_bias, c_conv, c_head_norm, c_w_out, ffn2_norm, ffn2_w_in, ffn2_w_out):
    p = _prepare(ffn1_norm, ffn1_w_in, ffn1_w_out, mix_norm, ab_w_in, ab_q_norm, ab_k_norm, ab_w_out,
                 c_w_in, c_gate_bias, c_conv, c_head_norm, c_w_out, ffn2_norm, ffn2_w_in, ffn2_w_out)
    return (_trunk(x_prompt, p), _trunk(x_sample, p))
```

```python
import functools

import jax
import jax.numpy as jnp
from jax import lax
from jax.experimental import pallas as pl
from jax.experimental.pallas import tpu as pltpu

F32 = jnp.float32
BF16 = jnp.bfloat16

D_MODEL = 1024
D_FF = 2816
EPS = 1e-6
LOG2E = 1.4426950408889634
GRID_W = 64
FOURIER_W = 256
FOURIER_CH = 64
HEAD_DIM = 64
ATTN_W = 768
N_KV_HEADS = 4
Q_PER_KV = 3
KV_W = 256
PAD_KV_W = 4 * 128
ROPE_THETA = 10000.0
ROPE_FREQS = 16
M_HEADS = 4
M_W = 1024
M_HEAD_DIM = 256
N_GATES = 4 * M_HEADS

V7X_VMEM_BYTES = 64 * 1024 * 1024
LANES = 128
SUBLANES = 8

TOKEN_TILE = 512
ATTN_Q_TILE = 256
ATTN_KV_TILE = 1024
DFT_TILE = 1024
DFT_PHASE_ROWS = 64
MAX_SOFTMAX_SHIFT = 40.0
MLSTM_CHUNK = 256
FFN_CHUNKS = 1
VMEM_LIMIT = V7X_VMEM_BYTES - 8 * 1024 * 1024


def _params(*semantics):
    return pltpu.CompilerParams(dimension_semantics=semantics, vmem_limit_bytes=VMEM_LIMIT)


def _resident(shape):
    zeros = (0,) * len(shape)
    return pl.BlockSpec(shape, lambda *_: zeros, pipeline_mode=pl.Buffered(1))


def _dot(a, b):
    return jnp.dot(a, b, preferred_element_type=F32)


def _dot_nt(a, b):
    return lax.dot_general(a, b, (((1,), (1,)), ((), ())), preferred_element_type=F32)


def _dot_tn(a, b):
    return lax.dot_general(a, b, (((0,), (0,)), ((), ())), preferred_element_type=F32)


def _rmsnorm_bf16(x, gain):
    ms = jnp.mean(x * x, axis=-1, keepdims=True)
    return (x * lax.rsqrt(ms + EPS) * gain).astype(BF16)


def _ffn_step(x, g_ref, win_ref, wout_ref):
    u = _rmsnorm_bf16(x, g_ref[...])
    fc = D_FF // FFN_CHUNKS
    acc = None
    for j in range(FFN_CHUNKS):
        gate = _dot(u, win_ref[:, j * fc:(j + 1) * fc])
        up = _dot(u, win_ref[:, D_FF + j * fc:D_FF + (j + 1) * fc])
        h = (gate * jax.nn.sigmoid(gate) * up).astype(BF16)
        part = _dot(h, wout_ref[j * fc:(j + 1) * fc, :])
        acc = part if acc is None else acc + part
    return x + 0.5 * acc


def _ffn_body(x_ref, g_ref, win_ref, wout_ref, o_ref):
    o_ref[...] = _ffn_step(x_ref[...], g_ref, win_ref, wout_ref)


def _ffn_specs():
    return [_resident((1, D_MODEL)), _resident((D_MODEL, 2 * D_FF)), _resident((D_FF, D_MODEL))]


def _ffn(x, gain, w_in, w_out):
    B, S, D = x.shape
    T = B * S
    tm = min(TOKEN_TILE, T)
    out = pl.pallas_call(
        _ffn_body,
        grid=(T // tm,),
        in_specs=[pl.BlockSpec((tm, D), lambda i: (i, 0))] + _ffn_specs(),
        out_specs=pl.BlockSpec((tm, D), lambda i: (i, 0)),
        out_shape=jax.ShapeDtypeStruct((T, D), F32),
        compiler_params=_params("parallel"),
        name="ffn",
    )(x.reshape(T, D), gain, w_in, w_out)
    return out.reshape(B, S, D)


def _rope(xn, cos, sin, low_half):
    partner = jnp.where(low_half, pltpu.roll(xn, LANES - ROPE_FREQS, 1), pltpu.roll(xn, ROPE_FREQS, 1))
    return xn * cos + partner * sin


def _ab_in_body(x_ref, g_ref, w_ref, qg_ref, kg_ref, bd_ref, cos_ref, sin_ref, dft_ref, qpad_ref, kvpad_ref,
                q_ref, k_ref, v_ref, y_ref, z_ref):
    u = _rmsnorm_bf16(x_ref[...], g_ref[...])
    z = _dot(u, w_ref[...])
    f = z[:, :FOURIER_W].astype(BF16)
    yz = _dot(f, dft_ref[...])
    y_ref[...] = yz[:, :FOURIER_W].astype(BF16)
    z_ref[...] = yz[:, FOURIER_W:].astype(BF16)

    cos = cos_ref[...]
    sin = sin_ref[...]
    lane = lax.broadcasted_iota(jnp.int32, (1, LANES), 1)
    low_half = (lane % (2 * ROPE_FREQS)) < ROPE_FREQS
    is_head = lane < HEAD_DIM
    bd = bd_ref[...]

    def norm_rope(xs, gain):
        ms = _dot((xs * xs).astype(BF16), bd)
        xn = xs * lax.rsqrt(ms + EPS) * gain
        return [_rope(xn[:, i * LANES:(i + 1) * LANES], cos, sin, low_half) for i in range(2)]

    def pad_heads(pairs, pad):
        blocks = []
        for pair in pairs:
            blocks.append(jnp.where(is_head, pair, pad).astype(BF16))
            blocks.append(jnp.where(is_head, pltpu.roll(pair, HEAD_DIM, 1), pad).astype(BF16))
        return blocks

    n_qt, _, tq, _ = q_ref.shape
    per_group = KV_W // HEAD_DIM
    for c in range(ATTN_W // KV_W):
        lo = FOURIER_W + c * KV_W
        blocks = pad_heads(norm_rope(z[:, lo:lo + KV_W], qg_ref[:, c * KV_W:(c + 1) * KV_W]), qpad_ref[...])
        for i, blk in enumerate(blocks):
            for t in range(n_qt):
                q_ref[t, per_group * c + i] = blk[t * tq:(t + 1) * tq]
    lo = FOURIER_W + ATTN_W
    k_ref[...] = jnp.concatenate(pad_heads(norm_rope(z[:, lo:lo + KV_W], kg_ref[...]), kvpad_ref[...]), axis=-1)
    zv = z[:, lo + KV_W:]
    v_ref[...] = jnp.concatenate(pad_heads([zv[:, :LANES], zv[:, LANES:]], kvpad_ref[...]), axis=-1)


def _ab_in(x, gain, w, qg, kg, bd, cos, sin, dft, qpad, kvpad):
    B, S, D = x.shape
    tm = min(TOKEN_TILE, S)
    tq = min(ATTN_Q_TILE, S)
    n_heads = ATTN_W // HEAD_DIM
    assert tm % tq == 0
    tok = lambda width: pl.BlockSpec((None, tm, width), lambda b, s: (b, s, 0))
    spec = lambda width: pl.BlockSpec((tm, width), lambda b, s: (s, b))
    q_spec = pl.BlockSpec((None, tm // tq, n_heads, tq, LANES), lambda b, s: (b, s, 0, 0, 0))
    return pl.pallas_call(
        _ab_in_body,
        grid=(B, S // tm),
        in_specs=[
            tok(D),
            _resident((1, D)),
            _resident(w.shape),
            _resident(qg.shape),
            _resident(kg.shape),
            _resident(bd.shape),
            pl.BlockSpec((tm, LANES), lambda b, s: (s, 0)),
            pl.BlockSpec((tm, LANES), lambda b, s: (s, 0)),
            _resident(dft.shape),
            _resident(qpad.shape),
            _resident(kvpad.shape),
        ],
        out_specs=[q_spec, tok(PAD_KV_W), tok(PAD_KV_W), spec(FOURIER_W), spec(FOURIER_W)],
        out_shape=[
            jax.ShapeDtypeStruct((B, S // tq, n_heads, tq, LANES), BF16),
            jax.ShapeDtypeStruct((B, S, PAD_KV_W), BF16),
            jax.ShapeDtypeStruct((B, S, PAD_KV_W), BF16),
            jax.ShapeDtypeStruct((S, B * FOURIER_W), BF16),
            jax.ShapeDtypeStruct((S, B * FOURIER_W), BF16),
        ],
        compiler_params=_params("parallel", "parallel"),
        name="ab_in",
    )(x, gain, w, qg, kg, bd, cos, sin, dft, qpad, kvpad)


def _dft_body(c_ref, s_ref, y_ref, z_ref, o_ref, acc_ref, *, scale):
    kk = pl.program_id(2)
    part = _dot(c_ref[...], y_ref[...]) + _dot(s_ref[...], z_ref[...])

    @pl.when(kk == 0)
    def _():
        acc_ref[...] = part

    @pl.when(kk > 0)
    def _():
        acc_ref[...] += part

    @pl.when(kk == pl.num_programs(2) - 1)
    def _():
        o_ref[...] = (acc_ref[...] * scale).astype(BF16)


def _seq_dft(cmat, smat, y, z):
    S, N = y.shape
    tm = min(DFT_TILE, S)
    tn = min(DFT_TILE, N)
    tk = min(DFT_TILE, S)
    return pl.pallas_call(
        functools.partial(_dft_body, scale=float(S) ** -0.5),
        grid=(S // tm, N // tn, S // tk),
        in_specs=[
            pl.BlockSpec((tm, tk), lambda i, j, k: (i, k)),
            pl.BlockSpec((tm, tk), lambda i, j, k: (i, k)),
            pl.BlockSpec((tk, tn), lambda i, j, k: (k, j)),
            pl.BlockSpec((tk, tn), lambda i, j, k: (k, j)),
        ],
        out_specs=pl.BlockSpec((tm, tn), lambda i, j, k: (i, j)),
        out_shape=jax.ShapeDtypeStruct((S, N), BF16),
        scratch_shapes=[pltpu.VMEM((tm, tn), F32)],
        compiler_params=_params("parallel", "parallel", "arbitrary"),
        name="seq_dft",
    )(cmat, smat, y, z)


def _attn_body(q_ref, k_ref, v_ref, o_ref, *, tk, online):
    tq = q_ref.shape[1]
    n_kv = k_ref.shape[0] // tk
    rows = Q_PER_KV * tq
    lane = lax.broadcasted_iota(jnp.int32, (1, LANES), 1)

    def step(i, carry):
        start = pl.multiple_of(i * tk, tk)
        out = []
        for j in range(N_KV_HEADS):
            qg = q_ref[Q_PER_KV * j:Q_PER_KV * (j + 1)].reshape(rows, LANES)
            ks = k_ref[pl.ds(start, tk), j * LANES:(j + 1) * LANES]
            vs = v_ref[pl.ds(start, tk), j * LANES:(j + 1) * LANES]
            if not online:
                out.append(carry[j] + _dot(jnp.exp(_dot_nt(qg, ks)).astype(BF16), vs))
                continue
            s = _dot_nt(jnp.where(lane == HEAD_DIM, jnp.zeros_like(qg), qg), ks)
            m, acc = carry[j]
            m_new = jnp.maximum(m, jnp.max(s, axis=-1, keepdims=True))
            p = jnp.exp(s - m_new).astype(BF16)
            out.append((m_new, jnp.exp(m - m_new) * acc + _dot(p, vs)))
        return tuple(out)

    acc0 = jnp.zeros((rows, LANES), F32)
    init = (jnp.full((rows, 1), -jnp.inf, F32), acc0) if online else acc0
    final = lax.fori_loop(0, n_kv, step, (init,) * N_KV_HEADS)
    for j in range(N_KV_HEADS):
        acc = final[j][1] if online else final[j]
        out = (acc[:, :HEAD_DIM] / acc[:, HEAD_DIM:HEAD_DIM + 1]).astype(BF16)
        for g in range(Q_PER_KV):
            h = Q_PER_KV * j + g
            o_ref[:, h * HEAD_DIM:(h + 1) * HEAD_DIM] = out[g * tq:(g + 1) * tq]


def _attention(q, k, v, online):
    B, n_qt, n_heads, tq, _ = q.shape
    S = k.shape[1]
    tk = min(ATTN_KV_TILE, S)
    return pl.pallas_call(
        functools.partial(_attn_body, tk=tk, online=online),
        grid=(B, n_qt),
        in_specs=[
            pl.BlockSpec((None, None, n_heads, tq, LANES), lambda b, i: (b, i, 0, 0, 0)),
            pl.BlockSpec((None, S, PAD_KV_W), lambda b, i: (b, 0, 0)),
            pl.BlockSpec((None, S, PAD_KV_W), lambda b, i: (b, 0, 0)),
        ],
        out_specs=pl.BlockSpec((None, tq, ATTN_W), lambda b, i: (b, i, 0)),
        out_shape=jax.ShapeDtypeStruct((B, S, ATTN_W), BF16),
        compiler_params=_params("parallel", "arbitrary"),
        name="gqa_online" if online else "gqa_shifted",
    )(q, k, v)


def _ab_out_body(x_ref, f_ref, a_ref, wf_ref, wa_ref, g_ref, win_ref, wout_ref, o_ref):
    x = x_ref[...] + _dot(f_ref[...], wf_ref[...]) + _dot(a_ref[...], wa_ref[...])
    o_ref[...] = _ffn_step(x, g_ref, win_ref, wout_ref)


def _ab_out_ffn(x, fmix, attn, wf, wa, ffn):
    B, S, D = x.shape
    tm = min(TOKEN_TILE, S)
    tok = lambda width: pl.BlockSpec((None, tm, width), lambda b, s: (b, s, 0))
    return pl.pallas_call(
        _ab_out_body,
        grid=(B, S // tm),
        in_specs=[
            tok(D),
            pl.BlockSpec((tm, FOURIER_W), lambda b, s: (s, b)),
            tok(ATTN_W),
            _resident(wf.shape),
            _resident(wa.shape),
        ] + _ffn_specs(),
        out_specs=tok(D),
        out_shape=jax.ShapeDtypeStruct((B, S, D), F32),
        compiler_params=_params("parallel", "parallel"),
        name="ab_out_ffn",
    )(x, fmix, attn, wf, wa, *ffn)


def _c_in_body(x_ref, xp_ref, xn_ref, g_ref, wqk_ref, wvo_ref, wg_ref, wgt_ref, bias_ref, biast_ref,
               cw_ref, qk_ref, v_ref, o_ref, gate_ref, gatet_ref):
    tm = x_ref.shape[0]
    rows = tm + 2 * SUBLANES
    s = pl.program_id(1)
    xe = jnp.concatenate([xp_ref[...], x_ref[...], xn_ref[...]], axis=0)
    ue = _rmsnorm_bf16(xe, g_ref[...])
    z = _dot(ue, wqk_ref[...])
    row = lax.broadcasted_iota(jnp.int32, (rows, 1), 0)
    outside = jnp.logical_or(jnp.logical_and(row < SUBLANES, s == 0),
                             jnp.logical_and(row >= tm + SUBLANES, s == pl.num_programs(1) - 1))
    z = jnp.where(outside, 0.0, z)
    mid = slice(SUBLANES, SUBLANES + tm)
    conv = (cw_ref[0:1, :] * pltpu.roll(z, 1, 0)[mid]
            + cw_ref[1:2, :] * z[mid]
            + cw_ref[2:3, :] * pltpu.roll(z, rows - 1, 0)[mid])
    act = conv * jax.nn.sigmoid(conv)
    qk_ref[:, :M_W] = act[:, :M_W].astype(BF16)
    qk_ref[:, M_W:] = (act[:, M_W:] * (float(M_HEAD_DIM) ** -0.5)).astype(BF16)

    u = ue[SUBLANES:SUBLANES + tm]
    zvo = _dot(u, wvo_ref[...])
    v_ref[...] = zvo[:, :M_W].astype(BF16)
    o_ref[...] = zvo[:, M_W:].astype(BF16)
    gate_ref[...] = _dot(u, wg_ref[...]) + bias_ref[...]
    gatet_ref[...] = _dot_nt(wgt_ref[...], u) + biast_ref[...]


def _c_in(x, gain, wqk, wvo, wg, wgt, bias, biast, cw):
    B, S, D = x.shape
    tm = min(TOKEN_TILE, S)
    per = tm // SUBLANES
    n_halo = S // SUBLANES
    tok = lambda width: pl.BlockSpec((None, tm, width), lambda b, s: (b, s, 0))
    return pl.pallas_call(
        _c_in_body,
        grid=(B, S // tm),
        in_specs=[
            tok(D),
            pl.BlockSpec((None, SUBLANES, D), lambda b, s: (b, jnp.maximum(s * per - 1, 0), 0)),
            pl.BlockSpec((None, SUBLANES, D), lambda b, s: (b, jnp.minimum((s + 1) * per, n_halo - 1), 0)),
            _resident((1, D)),
            _resident(wqk.shape),
            _resident(wvo.shape),
            _resident(wg.shape),
            _resident(wgt.shape),
            _resident(bias.shape),
            _resident(biast.shape),
            _resident(cw.shape),
        ],
        out_specs=[tok(2 * M_W), tok(M_W), tok(M_W), tok(LANES),
                   pl.BlockSpec((None, N_GATES, tm), lambda b, s: (b, 0, s))],
        out_shape=[
            jax.ShapeDtypeStruct((B, S, 2 * M_W), BF16),
            jax.ShapeDtypeStruct((B, S, M_W), BF16),
            jax.ShapeDtypeStruct((B, S, M_W), BF16),
            jax.ShapeDtypeStruct((B, S, LANES), F32),
            jax.ShapeDtypeStruct((B, N_GATES, S), F32),
        ],
        compiler_params=_params("parallel", "parallel"),
        name="c_in",
    )(x, x, x, gain, wqk, wvo, wg, wgt, bias, biast, cw)


def _split3(x):
    hi = x.astype(BF16)
    r1 = x - hi.astype(F32)
    mid = r1.astype(BF16)
    lo = (r1 - mid.astype(F32)).astype(BF16)
    return hi, mid, lo


def _log_sigmoid(x):
    return jnp.minimum(x, 0.0) - jnp.log1p(jnp.exp(-jnp.abs(x)))


def _mlstm_dir(qk_ref, v_ref, g_ref, gt_ref, h_ref, a_ref, m_ref, *, backward):
    L = qk_ref.shape[0]
    t_idx = lax.broadcasted_iota(jnp.int32, (L, L), 0)
    s_idx = lax.broadcasted_iota(jnp.int32, (L, L), 1)
    keep = (s_idx >= t_idx) if backward else (s_idx <= t_idx)
    tri = jnp.where(keep, 1.0, 0.0).astype(BF16)
    tri_t = jnp.where((t_idx >= s_idx) if backward else (t_idx <= s_idx), 1.0, 0.0).astype(BF16)
    end = 0 if backward else L - 1
    d = 1 if backward else 0
    ones_blk = jnp.ones((L, LANES), BF16)
    lane_tiles = L // LANES

    gates = g_ref[...]
    gates_t = gt_ref[...]
    cum = sum(_dot(tri, piece) for piece in _split3(_log_sigmoid(gates)))
    cum_t = sum(_dot(piece, tri_t) for piece in _split3(_log_sigmoid(gates_t)))

    def lanes(col):
        return jnp.broadcast_to(col, (L, LANES))

    for h in range(M_HEADS):
        ci = 2 * d * M_HEADS + h
        cf = ci + M_HEADS
        r = d * M_HEADS + h
        q = qk_ref[:, h * M_HEAD_DIM:(h + 1) * M_HEAD_DIM]
        k = qk_ref[:, M_W + h * M_HEAD_DIM:M_W + (h + 1) * M_HEAD_DIM]
        v_aug = jnp.concatenate([v_ref[:, h * M_HEAD_DIM:(h + 1) * M_HEAD_DIM], ones_blk], axis=-1)
        b_b = lanes(cum[:, cf:cf + 1]) * LOG2E
        a_b = lanes(gates[:, ci:ci + 1]) * LOG2E - b_b
        a_row = (gates_t[ci:ci + 1, :] - cum_t[cf:cf + 1, :]) * LOG2E
        m = m_ref[r:r + 1, 0:1]
        m2 = m * LOG2E
        state = a_ref[r]

        a_mask = jnp.where(keep, a_row, -jnp.inf)
        g2 = jnp.maximum(m2, jnp.max(a_mask, axis=-1, keepdims=True))
        g2_b = lanes(g2)
        sqk = (_dot_nt(q, k) * jnp.exp2(a_mask - jnp.tile(g2_b, (1, lane_tiles)))).astype(BF16)
        w_b = jnp.exp2(m2 - g2_b).astype(BF16)
        qw = q * jnp.tile(w_b, (1, M_HEAD_DIM // LANES))
        res = _dot(sqk, v_aug) + _dot(qw, state.astype(BF16))
        floor_b = jnp.exp2(-b_b - g2_b)
        scale_b = 1.0 / jnp.maximum(jnp.abs(res[:, M_HEAD_DIM:]), floor_b)
        h_ref[:, h * M_HEAD_DIM:(h + 1) * M_HEAD_DIM] = (
            res[:, :M_HEAD_DIM] * jnp.tile(scale_b, (1, M_HEAD_DIM // LANES))).astype(BF16)

        g2_end = g2[end:end + 1, :]
        ws_b = jnp.exp2(a_b - g2_end).astype(BF16)
        a_ref[r] = jnp.exp2(m2 - g2_end) * state + _dot_tn(k, v_aug * jnp.tile(ws_b, (1, 3)))
        m_new = cum[end:end + 1, cf:cf + 1] + g2_end * (1.0 / LOG2E)
        m_ref[r:r + 1, :] = jnp.broadcast_to(m_new, (1, LANES))


def _mlstm_body(qkf_ref, vf_ref, gf_ref, gtf_ref, qkb_ref, vb_ref, gb_ref, gtb_ref,
                hf_ref, hb_ref, a_ref, m_ref):
    @pl.when(pl.program_id(1) == 0)
    def _():
        a_ref[...] = jnp.zeros_like(a_ref)
        m_ref[...] = jnp.zeros_like(m_ref)

    _mlstm_dir(qkf_ref, vf_ref, gf_ref, gtf_ref, hf_ref, a_ref, m_ref, backward=False)
    _mlstm_dir(qkb_ref, vb_ref, gb_ref, gtb_ref, hb_ref, a_ref, m_ref, backward=True)


def _mlstm(qk, v, gates, gates_t):
    B, S, _ = v.shape
    L = min(MLSTM_CHUNK, S)
    nc = S // L
    fwd = lambda width: pl.BlockSpec((None, L, width), lambda b, c: (b, c, 0))
    bwd = lambda width: pl.BlockSpec((None, L, width), lambda b, c: (b, nc - 1 - c, 0))
    return pl.pallas_call(
        _mlstm_body,
        grid=(B, nc),
        in_specs=[
            fwd(2 * M_W), fwd(M_W), fwd(LANES),
            pl.BlockSpec((None, N_GATES, L), lambda b, c: (b, 0, c)),
            bwd(2 * M_W), bwd(M_W), bwd(LANES),
            pl.BlockSpec((None, N_GATES, L), lambda b, c: (b, 0, nc - 1 - c)),
        ],
        out_specs=[fwd(M_W), bwd(M_W)],
        out_shape=[jax.ShapeDtypeStruct((B, S, M_W), BF16)] * 2,
        scratch_shapes=[
            pltpu.VMEM((2 * M_HEADS, M_HEAD_DIM, M_HEAD_DIM + LANES), F32),
            pltpu.VMEM((2 * M_HEADS, LANES), F32),
        ],
        compiler_params=_params("parallel", "arbitrary"),
        name="mlstm",
    )(qk, v, gates, gates_t, qk, v, gates, gates_t)


def _c_out_body(x_ref, hf_ref, hb_ref, o_ref, hg_ref, w_ref, g_ref, win_ref, wout_ref, out_ref):
    h = hf_ref[...].astype(F32) + hb_ref[...].astype(F32)
    normed = []
    for i in range(M_HEADS):
        hh = h[:, i * M_HEAD_DIM:(i + 1) * M_HEAD_DIM]
        ms = jnp.mean(hh * hh, axis=-1, keepdims=True)
        normed.append(hh * lax.rsqrt(ms + EPS))
    hn = jnp.concatenate(normed, axis=-1) * hg_ref[...]
    gated = (jax.nn.sigmoid(o_ref[...].astype(F32)) * hn).astype(BF16)
    x = x_ref[...] + _dot(gated, w_ref[...])
    out_ref[...] = _ffn_step(x, g_ref, win_ref, wout_ref)


def _c_out_ffn(x, hf, hb, o, hg, w, ffn):
    B, S, D = x.shape
    tm = min(TOKEN_TILE, S)
    tok = lambda width: pl.BlockSpec((None, tm, width), lambda b, s: (b, s, 0))
    return pl.pallas_call(
        _c_out_body,
        grid=(B, S // tm),
        in_specs=[tok(D), tok(M_W), tok(M_W), tok(M_W), _resident((1, M_W)), _resident(w.shape)] + _ffn_specs(),
        out_specs=tok(D),
        out_shape=jax.ShapeDtypeStruct((B, S, D), F32),
        compiler_params=_params("parallel", "parallel"),
        name="c_out_ffn",
    )(x, hf, hb, o, hg, w, *ffn)


def _rope_tables(S):
    t = jnp.arange(S, dtype=jnp.int32)
    row_id = (t // GRID_W).astype(F32)
    col_id = (t % GRID_W).astype(F32)
    freqs = ROPE_THETA ** (-jnp.arange(ROPE_FREQS, dtype=F32) / ROPE_FREQS)
    ang_r = row_id[:, None] * freqs[None, :]
    ang_c = col_id[:, None] * freqs[None, :]
    cos = jnp.concatenate([jnp.cos(ang_r)] * 2 + [jnp.cos(ang_c)] * 2, axis=-1)
    sin = jnp.concatenate([-jnp.sin(ang_r), jnp.sin(ang_r), -jnp.sin(ang_c), jnp.sin(ang_c)], axis=-1)
    return jnp.tile(cos, (1, LANES // HEAD_DIM)), jnp.tile(sin, (1, LANES // HEAD_DIM))


def _dft_matrices(n):
    j = jnp.arange(n, dtype=jnp.int32)
    phase = ((j[:, None] * j[None, :]) % n).astype(F32) * (2.0 * jnp.pi / n)
    return jnp.cos(phase), -jnp.sin(phase)


def _seq_dft_matrices(n):
    rows = min(DFT_PHASE_ROWS, n)
    t = jnp.arange(n, dtype=jnp.int32)
    unit = 2.0 * jnp.pi / n
    pa = ((jnp.arange(0, n, rows, dtype=jnp.int32)[:, None] * t[None, :]) % n).astype(F32) * unit
    pb = ((jnp.arange(rows, dtype=jnp.int32)[:, None] * t[None, :]) % n).astype(F32) * unit
    ca, sa = jnp.cos(pa)[:, None, :], jnp.sin(pa)[:, None, :]
    cb, sb = jnp.cos(pb)[None], jnp.sin(pb)[None]
    cmat = (ca * cb - sa * sb).reshape(n, n).astype(BF16)
    smat = (-(sa * cb + ca * sb)).reshape(n, n).astype(BF16)
    return cmat, smat


def _block_diag(block, copies):
    return jnp.kron(jnp.eye(copies, dtype=block.dtype), block)


def _prepare(ffn1_norm, ffn1_w_in, ffn1_w_out, mix_norm, ab_w_in, ab_q_norm, ab_k_norm, ab_w_out,
             c_w_in, c_gate_bias, c_conv, c_head_norm, c_w_out, ffn2_norm, ffn2_w_in, ffn2_w_out):
    groups = FOURIER_W // FOURIER_CH
    c64, s64 = _dft_matrices(FOURIER_CH)
    ch_scale = float(FOURIER_CH) ** -0.5
    shift = 1.01 * float(HEAD_DIM) ** 0.5 * jnp.max(jnp.abs(ab_q_norm[0])) * jnp.max(jnp.abs(ab_k_norm[0]))
    p = {
        "ffn1": [(ffn1_norm[l][None], ffn1_w_in[l].astype(BF16), ffn1_w_out[l].astype(BF16)) for l in range(2)],
        "ffn2": [(ffn2_norm[l][None], ffn2_w_in[l].astype(BF16), ffn2_w_out[l].astype(BF16)) for l in range(2)],
        "mix_norm": [mix_norm[l][None] for l in range(2)],
        "ab_w_in": ab_w_in[0].astype(BF16),
        "qg": jnp.tile(ab_q_norm[0], ATTN_W // HEAD_DIM)[None] * (float(HEAD_DIM) ** -0.5),
        "kg": jnp.tile(ab_k_norm[0], KV_W // HEAD_DIM)[None],
        "bd": _block_diag(jnp.full((HEAD_DIM, HEAD_DIM), 1.0 / HEAD_DIM, F32), KV_W // HEAD_DIM).astype(BF16),
        "dft64": jnp.concatenate([_block_diag(c64 * ch_scale, groups),
                                  _block_diag(-s64 * ch_scale, groups)], axis=-1).astype(BF16),
        "ab_wf": ab_w_out[0][:FOURIER_W].astype(BF16),
        "ab_wa": ab_w_out[0][FOURIER_W:].astype(BF16),
        "c_wqk": c_w_in[0][:, :2 * M_W].astype(BF16),
        "c_wvo": c_w_in[0][:, 2 * M_W:4 * M_W].astype(BF16),
        "c_wg": jnp.pad(c_w_in[0][:, 4 * M_W:], ((0, 0), (0, LANES - N_GATES))).astype(BF16),
        "c_wgt": c_w_in[0][:, 4 * M_W:].T.astype(BF16),
        "c_bias": jnp.pad(c_gate_bias[0], (0, LANES - N_GATES))[None],
        "c_biast": c_gate_bias[0][:, None],
        "c_conv": c_conv[0],
        "c_hg": c_head_norm[0][None],
        "shift": shift,
        "qpad": jnp.zeros((1, LANES), F32).at[0, HEAD_DIM].set(-shift),
        "kvpad": jnp.zeros((1, LANES), F32).at[0, HEAD_DIM].set(1.0),
        "c_wout": c_w_out[0].astype(BF16),
    }
    return p


def _trunk(x, p):
    B, S, _ = x.shape
    x = _ffn(x, *p["ffn1"][0])
    cos, sin = _rope_tables(S)
    q, k, v, y, z = _ab_in(x, p["mix_norm"][0], p["ab_w_in"], p["qg"], p["kg"], p["bd"], cos, sin, p["dft64"],
                           p["qpad"], p["kvpad"])
    cmat, smat = _seq_dft_matrices(S)
    fmix = _seq_dft(cmat, smat, y, z)
    attn = lax.cond(p["shift"] <= MAX_SOFTMAX_SHIFT,
                    functools.partial(_attention, online=False),
                    functools.partial(_attention, online=True), q, k, v)
    x = _ab_out_ffn(x, fmix, attn, p["ab_wf"], p["ab_wa"], p["ffn2"][0])
    x = _ffn(x, *p["ffn1"][1])
    qk, v, o, gates, gates_t = _c_in(x, p["mix_norm"][1], p["c_wqk"], p["c_wvo"], p["c_wg"], p["c_wgt"],
                                     p["c_bias"], p["c_biast"], p["c_conv"])
    hf, hb = _mlstm(qk, v, gates, gates_t)
    x = _c_out_ffn(x, hf, hb, o, p["c_hg"], p["c_wout"], p["ffn2"][1])
    return x


def kernel(x_prompt, x_sample, ffn1_norm, ffn1_w_in, ffn1_w_out, mix_norm, ab_w_in, ab_q_norm, ab_k_norm,
           ab_w_out, c_w_in, c_gate_bias, c_conv, c_head_norm, c_w_out, ffn2_norm, ffn2_w_in, ffn2_w_out):
    p = _prepare(ffn1_norm, ffn1_w_in, ffn1_w_out, mix_norm, ab_w_in, ab_q_norm, ab_k_norm, ab_w_out,
                 c_w_in, c_gate_bias, c_conv, c_head_norm, c_w_out, ffn2_norm, ffn2_w_in, ffn2_w_out)
    return (_trunk(x_prompt, p), _trunk(x_sample, p))
```

```python
import functools

import jax
import jax.numpy as jnp
from jax import lax
from jax.experimental import pallas as pl
from jax.experimental.pallas import tpu as pltpu

F32 = jnp.float32
BF16 = jnp.bfloat16

D_MODEL = 1024
D_FF = 2816
EPS = 1e-6
LOG2E = 1.4426950408889634
GRID_W = 64
FOURIER_W = 256
FOURIER_CH = 64
HEAD_DIM = 64
ATTN_W = 768
N_KV_HEADS = 4
Q_PER_KV = 3
KV_W = 256
PAD_KV_W = 4 * 128
V_ROWS = 80
ROPE_THETA = 10000.0
ROPE_FREQS = 16
M_HEADS = 4
M_W = 1024
M_HEAD_DIM = 256
N_GATES = 4 * M_HEADS

V7X_VMEM_BYTES = 64 * 1024 * 1024
LANES = 128
SUBLANES = 8

TOKEN_TILE = 512
ATTN_Q_TILE = 256
ATTN_KV_TILE = 1024
DFT_TILE = 1024
DFT_PHASE_ROWS = 64
MAX_SOFTMAX_SHIFT = 40.0
MLSTM_CHUNK = 256
FFN_CHUNKS = 1
VMEM_LIMIT = V7X_VMEM_BYTES - 8 * 1024 * 1024


def _params(*semantics):
    return pltpu.CompilerParams(dimension_semantics=semantics, vmem_limit_bytes=VMEM_LIMIT)


def _resident(shape):
    zeros = (0,) * len(shape)
    return pl.BlockSpec(shape, lambda *_: zeros, pipeline_mode=pl.Buffered(1))


def _dot(a, b):
    return jnp.dot(a, b, preferred_element_type=F32)


def _dot_nt(a, b):
    return lax.dot_general(a, b, (((1,), (1,)), ((), ())), preferred_element_type=F32)


def _dot_tn(a, b):
    return lax.dot_general(a, b, (((0,), (0,)), ((), ())), preferred_element_type=F32)


def _rmsnorm_bf16(x, gain):
    ms = jnp.mean(x * x, axis=-1, keepdims=True)
    return (x * lax.rsqrt(ms + EPS) * gain).astype(BF16)


def _ffn_step(x, g_ref, win_ref, wout_ref):
    u = _rmsnorm_bf16(x, g_ref[...])
    fc = D_FF // FFN_CHUNKS
    acc = None
    for j in range(FFN_CHUNKS):
        gate = _dot(u, win_ref[:, j * fc:(j + 1) * fc])
        up = _dot(u, win_ref[:, D_FF + j * fc:D_FF + (j + 1) * fc])
        h = (gate * jax.nn.sigmoid(gate) * up).astype(BF16)
        part = _dot(h, wout_ref[j * fc:(j + 1) * fc, :])
        acc = part if acc is None else acc + part
    return x + 0.5 * acc


def _ffn_body(x_ref, g_ref, win_ref, wout_ref, o_ref):
    o_ref[...] = _ffn_step(x_ref[...], g_ref, win_ref, wout_ref)


def _ffn_specs():
    return [_resident((1, D_MODEL)), _resident((D_MODEL, 2 * D_FF)), _resident((D_FF, D_MODEL))]


def _ffn(x, gain, w_in, w_out):
    B, S, D = x.shape
    T = B * S
    tm = min(TOKEN_TILE, T)
    out = pl.pallas_call(
        _ffn_body,
        grid=(T // tm,),
        in_specs=[pl.BlockSpec((tm, D), lambda i: (i, 0))] + _ffn_specs(),
        out_specs=pl.BlockSpec((tm, D), lambda i: (i, 0)),
        out_shape=jax.ShapeDtypeStruct((T, D), F32),
        compiler_params=_params("parallel"),
        name="ffn",
    )(x.reshape(T, D), gain, w_in, w_out)
    return out.reshape(B, S, D)


def _rope(xn, cos, sin, low_half):
    partner = jnp.where(low_half, pltpu.roll(xn, LANES - ROPE_FREQS, 1), pltpu.roll(xn, ROPE_FREQS, 1))
    return xn * cos + partner * sin


def _ab_in_body(x_ref, g_ref, w_ref, qg_ref, kg_ref, bd_ref, cos_ref, sin_ref, dft_ref, qpad_ref, kvpad_ref,
                q_ref, k_ref, v_ref, y_ref, z_ref):
    u = _rmsnorm_bf16(x_ref[...], g_ref[...])
    z = _dot(u, w_ref[...])
    f = z[:, :FOURIER_W].astype(BF16)
    yz = _dot(f, dft_ref[...])
    y_ref[...] = yz[:, :FOURIER_W].astype(BF16)
    z_ref[...] = yz[:, FOURIER_W:].astype(BF16)

    cos = cos_ref[...]
    sin = sin_ref[...]
    lane = lax.broadcasted_iota(jnp.int32, (1, LANES), 1)
    low_half = (lane % (2 * ROPE_FREQS)) < ROPE_FREQS
    is_head = lane < HEAD_DIM
    bd = bd_ref[...]

    def norm_rope(xs, gain):
        ms = _dot((xs * xs).astype(BF16), bd)
        xn = xs * lax.rsqrt(ms + EPS) * gain
        return [_rope(xn[:, i * LANES:(i + 1) * LANES], cos, sin, low_half) for i in range(2)]

    def pad_heads(pairs, pad):
        blocks = []
        for pair in pairs:
            blocks.append(jnp.where(is_head, pair, pad).astype(BF16))
            blocks.append(jnp.where(is_head, pltpu.roll(pair, HEAD_DIM, 1), pad).astype(BF16))
        return blocks

    n_qt, _, tq, _ = q_ref.shape
    per_group = KV_W // HEAD_DIM
    for c in range(ATTN_W // KV_W):
        lo = FOURIER_W + c * KV_W
        blocks = pad_heads(norm_rope(z[:, lo:lo + KV_W], qg_ref[:, c * KV_W:(c + 1) * KV_W]), qpad_ref[...])
        for i, blk in enumerate(blocks):
            for t in range(n_qt):
                q_ref[t, per_group * c + i] = blk[t * tq:(t + 1) * tq]
    lo = FOURIER_W + ATTN_W
    k_ref[...] = jnp.concatenate(pad_heads(norm_rope(z[:, lo:lo + KV_W], kg_ref[...]), kvpad_ref[...]), axis=-1)
    zvt = z[:, lo + KV_W:].T
    tm = zvt.shape[1]
    tail = jnp.where(lax.broadcasted_iota(jnp.int32, (V_ROWS - HEAD_DIM, tm), 0) == 0, 1.0, 0.0)
    for j in range(N_KV_HEADS):
        v_ref[j] = jnp.concatenate([zvt[j * HEAD_DIM:(j + 1) * HEAD_DIM], tail], axis=0).astype(BF16)


def _ab_in(x, gain, w, qg, kg, bd, cos, sin, dft, qpad, kvpad):
    B, S, D = x.shape
    tm = min(TOKEN_TILE, S)
    tq = min(ATTN_Q_TILE, S)
    n_heads = ATTN_W // HEAD_DIM
    assert tm % tq == 0
    tok = lambda width: pl.BlockSpec((None, tm, width), lambda b, s: (b, s, 0))
    spec = lambda width: pl.BlockSpec((tm, width), lambda b, s: (s, b))
    q_spec = pl.BlockSpec((None, tm // tq, n_heads, tq, LANES), lambda b, s: (b, s, 0, 0, 0))
    return pl.pallas_call(
        _ab_in_body,
        grid=(B, S // tm),
        in_specs=[
            tok(D),
            _resident((1, D)),
            _resident(w.shape),
            _resident(qg.shape),
            _resident(kg.shape),
            _resident(bd.shape),
            pl.BlockSpec((tm, LANES), lambda b, s: (s, 0)),
            pl.BlockSpec((tm, LANES), lambda b, s: (s, 0)),
            _resident(dft.shape),
            _resident(qpad.shape),
            _resident(kvpad.shape),
        ],
        out_specs=[q_spec, tok(PAD_KV_W), pl.BlockSpec((None, N_KV_HEADS, V_ROWS, tm), lambda b, s: (b, 0, 0, s)),
                   spec(FOURIER_W), spec(FOURIER_W)],
        out_shape=[
            jax.ShapeDtypeStruct((B, S // tq, n_heads, tq, LANES), BF16),
            jax.ShapeDtypeStruct((B, S, PAD_KV_W), BF16),
            jax.ShapeDtypeStruct((B, N_KV_HEADS, V_ROWS, S), BF16),
            jax.ShapeDtypeStruct((S, B * FOURIER_W), BF16),
            jax.ShapeDtypeStruct((S, B * FOURIER_W), BF16),
        ],
        compiler_params=_params("parallel", "parallel"),
        name="ab_in",
    )(x, gain, w, qg, kg, bd, cos, sin, dft, qpad, kvpad)


def _dft_body(c_ref, s_ref, y_ref, z_ref, o_ref, acc_ref, *, scale):
    kk = pl.program_id(2)
    part = _dot(c_ref[...], y_ref[...]) + _dot(s_ref[...], z_ref[...])

    @pl.when(kk == 0)
    def _():
        acc_ref[...] = part

    @pl.when(kk > 0)
    def _():
        acc_ref[...] += part

    @pl.when(kk == pl.num_programs(2) - 1)
    def _():
        o_ref[...] = (acc_ref[...] * scale).astype(BF16)


def _seq_dft(cmat, smat, y, z):
    S, N = y.shape
    tm = min(DFT_TILE, S)
    tn = min(DFT_TILE, N)
    tk = min(DFT_TILE, S)
    return pl.pallas_call(
        functools.partial(_dft_body, scale=float(S) ** -0.5),
        grid=(S // tm, N // tn, S // tk),
        in_specs=[
            pl.BlockSpec((tm, tk), lambda i, j, k: (i, k)),
            pl.BlockSpec((tm, tk), lambda i, j, k: (i, k)),
            pl.BlockSpec((tk, tn), lambda i, j, k: (k, j)),
            pl.BlockSpec((tk, tn), lambda i, j, k: (k, j)),
        ],
        out_specs=pl.BlockSpec((tm, tn), lambda i, j, k: (i, j)),
        out_shape=jax.ShapeDtypeStruct((S, N), BF16),
        scratch_shapes=[pltpu.VMEM((tm, tn), F32)],
        compiler_params=_params("parallel", "parallel", "arbitrary"),
        name="seq_dft",
    )(cmat, smat, y, z)


def _attn_body(q_ref, k_ref, vt_ref, o_ref, *, tk, online):
    tq = q_ref.shape[1]
    n_kv = k_ref.shape[0] // tk
    rows = Q_PER_KV * tq
    lane = lax.broadcasted_iota(jnp.int32, (1, LANES), 1)

    def step(i, carry):
        start = pl.multiple_of(i * tk, tk)
        out = []
        for j in range(N_KV_HEADS):
            qg = q_ref[Q_PER_KV * j:Q_PER_KV * (j + 1)].reshape(rows, LANES)
            ks = k_ref[pl.ds(start, tk), j * LANES:(j + 1) * LANES]
            vt = vt_ref[j, :, pl.ds(start, tk)]
            if not online:
                out.append(carry[j] + _dot(vt, jnp.exp(_dot_nt(ks, qg)).astype(BF16)))
                continue
            st = _dot_nt(ks, jnp.where(lane == HEAD_DIM, jnp.zeros_like(qg), qg))
            m, acc = carry[j]
            m_new = jnp.maximum(m, jnp.max(st, axis=0, keepdims=True))
            p = jnp.exp(st - m_new).astype(BF16)
            out.append((m_new, jnp.exp(m - m_new) * acc + _dot(vt, p)))
        return tuple(out)

    acc0 = jnp.zeros((V_ROWS, rows), F32)
    init = (jnp.full((1, rows), -jnp.inf, F32), acc0) if online else acc0
    final = lax.fori_loop(0, n_kv, step, (init,) * N_KV_HEADS)
    for j in range(N_KV_HEADS):
        acc = final[j][1] if online else final[j]
        out_t = acc[:HEAD_DIM] / acc[HEAD_DIM:HEAD_DIM + 1]
        out = jnp.concatenate([out_t, jnp.zeros_like(out_t)], axis=0).T
        for g in range(Q_PER_KV):
            h = Q_PER_KV * j + g
            o_ref[:, h * HEAD_DIM:(h + 1) * HEAD_DIM] = out[g * tq:(g + 1) * tq, :HEAD_DIM].astype(BF16)


def _attention(q, k, v, online):
    B, n_qt, n_heads, tq, _ = q.shape
    S = k.shape[1]
    tk = min(ATTN_KV_TILE, S)
    return pl.pallas_call(
        functools.partial(_attn_body, tk=tk, online=online),
        grid=(B, n_qt),
        in_specs=[
            pl.BlockSpec((None, None, n_heads, tq, LANES), lambda b, i: (b, i, 0, 0, 0)),
            pl.BlockSpec((None, S, PAD_KV_W), lambda b, i: (b, 0, 0)),
            pl.BlockSpec((None, N_KV_HEADS, V_ROWS, S), lambda b, i: (b, 0, 0, 0)),
        ],
        out_specs=pl.BlockSpec((None, tq, ATTN_W), lambda b, i: (b, i, 0)),
        out_shape=jax.ShapeDtypeStruct((B, S, ATTN_W), BF16),
        compiler_params=_params("parallel", "arbitrary"),
        name="gqa_online" if online else "gqa_shifted",
    )(q, k, v)


def _ab_out_body(x_ref, f_ref, a_ref, wf_ref, wa_ref, g_ref, win_ref, wout_ref, o_ref):
    x = x_ref[...] + _dot(f_ref[...], wf_ref[...]) + _dot(a_ref[...], wa_ref[...])
    o_ref[...] = _ffn_step(x, g_ref, win_ref, wout_ref)


def _ab_out_ffn(x, fmix, attn, wf, wa, ffn):
    B, S, D = x.shape
    tm = min(TOKEN_TILE, S)
    tok = lambda width: pl.BlockSpec((None, tm, width), lambda b, s: (b, s, 0))
    return pl.pallas_call(
        _ab_out_body,
        grid=(B, S // tm),
        in_specs=[
            tok(D),
            pl.BlockSpec((tm, FOURIER_W), lambda b, s: (s, b)),
            tok(ATTN_W),
            _resident(wf.shape),
            _resident(wa.shape),
        ] + _ffn_specs(),
        out_specs=tok(D),
        out_shape=jax.ShapeDtypeStruct((B, S, D), F32),
        compiler_params=_params("parallel", "parallel"),
        name="ab_out_ffn",
    )(x, fmix, attn, wf, wa, *ffn)


def _c_in_body(x_ref, xp_ref, xn_ref, g_ref, wqk_ref, wvo_ref, wg_ref, wgt_ref, bias_ref, biast_ref,
               cw_ref, qk_ref, v_ref, o_ref, gate_ref, gatet_ref):
    tm = x_ref.shape[0]
    rows = tm + 2 * SUBLANES
    s = pl.program_id(1)
    xe = jnp.concatenate([xp_ref[...], x_ref[...], xn_ref[...]], axis=0)
    ue = _rmsnorm_bf16(xe, g_ref[...])
    z = _dot(ue, wqk_ref[...])
    row = lax.broadcasted_iota(jnp.int32, (rows, 1), 0)
    outside = jnp.logical_or(jnp.logical_and(row < SUBLANES, s == 0),
                             jnp.logical_and(row >= tm + SUBLANES, s == pl.num_programs(1) - 1))
    z = jnp.where(outside, 0.0, z)
    mid = slice(SUBLANES, SUBLANES + tm)
    conv = (cw_ref[0:1, :] * pltpu.roll(z, 1, 0)[mid]
            + cw_ref[1:2, :] * z[mid]
            + cw_ref[2:3, :] * pltpu.roll(z, rows - 1, 0)[mid])
    act = conv * jax.nn.sigmoid(conv)
    qk_ref[:, :M_W] = act[:, :M_W].astype(BF16)
    qk_ref[:, M_W:] = (act[:, M_W:] * (float(M_HEAD_DIM) ** -0.5)).astype(BF16)

    u = ue[SUBLANES:SUBLANES + tm]
    zvo = _dot(u, wvo_ref[...])
    v_ref[...] = zvo[:, :M_W].astype(BF16)
    o_ref[...] = zvo[:, M_W:].astype(BF16)
    gate_ref[...] = _dot(u, wg_ref[...]) + bias_ref[...]
    gatet_ref[...] = _dot_nt(wgt_ref[...], u) + biast_ref[...]


def _c_in(x, gain, wqk, wvo, wg, wgt, bias, biast, cw):
    B, S, D = x.shape
    tm = min(TOKEN_TILE, S)
    per = tm // SUBLANES
    n_halo = S // SUBLANES
    tok = lambda width: pl.BlockSpec((None, tm, width), lambda b, s: (b, s, 0))
    return pl.pallas_call(
        _c_in_body,
        grid=(B, S // tm),
        in_specs=[
            tok(D),
            pl.BlockSpec((None, SUBLANES, D), lambda b, s: (b, jnp.maximum(s * per - 1, 0), 0)),
            pl.BlockSpec((None, SUBLANES, D), lambda b, s: (b, jnp.minimum((s + 1) * per, n_halo - 1), 0)),
            _resident((1, D)),
            _resident(wqk.shape),
            _resident(wvo.shape),
            _resident(wg.shape),
            _resident(wgt.shape),
            _resident(bias.shape),
            _resident(biast.shape),
            _resident(cw.shape),
        ],
        out_specs=[tok(2 * M_W), tok(M_W), tok(M_W), tok(LANES),
                   pl.BlockSpec((None, N_GATES, tm), lambda b, s: (b, 0, s))],
        out_shape=[
            jax.ShapeDtypeStruct((B, S, 2 * M_W), BF16),
            jax.ShapeDtypeStruct((B, S, M_W), BF16),
            jax.ShapeDtypeStruct((B, S, M_W), BF16),
            jax.ShapeDtypeStruct((B, S, LANES), F32),
            jax.ShapeDtypeStruct((B, N_GATES, S), F32),
        ],
        compiler_params=_params("parallel", "parallel"),
        name="c_in",
    )(x, x, x, gain, wqk, wvo, wg, wgt, bias, biast, cw)


def _split3(x):
    hi = x.astype(BF16)
    r1 = x - hi.astype(F32)
    mid = r1.astype(BF16)
    lo = (r1 - mid.astype(F32)).astype(BF16)
    return hi, mid, lo


def _log_sigmoid(x):
    return jnp.minimum(x, 0.0) - jnp.log1p(jnp.exp(-jnp.abs(x)))


def _mlstm_dir(qk_ref, v_ref, g_ref, gt_ref, h_ref, a_ref, m_ref, *, backward):
    L = qk_ref.shape[0]
    t_idx = lax.broadcasted_iota(jnp.int32, (L, L), 0)
    s_idx = lax.broadcasted_iota(jnp.int32, (L, L), 1)
    keep = (s_idx >= t_idx) if backward else (s_idx <= t_idx)
    tri = jnp.where(keep, 1.0, 0.0).astype(BF16)
    tri_t = jnp.where((t_idx >= s_idx) if backward else (t_idx <= s_idx), 1.0, 0.0).astype(BF16)
    end = 0 if backward else L - 1
    d = 1 if backward else 0
    ones_blk = jnp.ones((L, LANES), BF16)
    lane_tiles = L // LANES

    gates = g_ref[...]
    gates_t = gt_ref[...]
    cum = sum(_dot(tri, piece) for piece in _split3(_log_sigmoid(gates)))
    cum_t = sum(_dot(piece, tri_t) for piece in _split3(_log_sigmoid(gates_t)))

    def lanes(col):
        return jnp.broadcast_to(col, (L, LANES))

    for h in range(M_HEADS):
        ci = 2 * d * M_HEADS + h
        cf = ci + M_HEADS
        r = d * M_HEADS + h
        q = qk_ref[:, h * M_HEAD_DIM:(h + 1) * M_HEAD_DIM]
        k = qk_ref[:, M_W + h * M_HEAD_DIM:M_W + (h + 1) * M_HEAD_DIM]
        v_aug = jnp.concatenate([v_ref[:, h * M_HEAD_DIM:(h + 1) * M_HEAD_DIM], ones_blk], axis=-1)
        b_b = lanes(cum[:, cf:cf + 1]) * LOG2E
        a_b = lanes(gates[:, ci:ci + 1]) * LOG2E - b_b
        a_row = (gates_t[ci:ci + 1, :] - cum_t[cf:cf + 1, :]) * LOG2E
        m = m_ref[r:r + 1, 0:1]
        m2 = m * LOG2E
        state = a_ref[r]

        a_mask = jnp.where(keep, a_row, -jnp.inf)
        g2 = jnp.maximum(m2, jnp.max(a_mask, axis=-1, keepdims=True))
        g2_b = lanes(g2)
        sqk = (_dot_nt(q, k) * jnp.exp2(a_mask - jnp.tile(g2_b, (1, lane_tiles)))).astype(BF16)
        w_b = jnp.exp2(m2 - g2_b).astype(BF16)
        qw = q * jnp.tile(w_b, (1, M_HEAD_DIM // LANES))
        res = _dot(sqk, v_aug) + _dot(qw, state.astype(BF16))
        floor_b = jnp.exp2(-b_b - g2_b)
        scale_b = 1.0 / jnp.maximum(jnp.abs(res[:, M_HEAD_DIM:]), floor_b)
        h_ref[:, h * M_HEAD_DIM:(h + 1) * M_HEAD_DIM] = (
            res[:, :M_HEAD_DIM] * jnp.tile(scale_b, (1, M_HEAD_DIM // LANES))).astype(BF16)

        g2_end = g2[end:end + 1, :]
        ws_b = jnp.exp2(a_b - g2_end).astype(BF16)
        a_ref[r] = jnp.exp2(m2 - g2_end) * state + _dot_tn(k, v_aug * jnp.tile(ws_b, (1, 3)))
        m_new = cum[end:end + 1, cf:cf + 1] + g2_end * (1.0 / LOG2E)
        m_ref[r:r + 1, :] = jnp.broadcast_to(m_new, (1, LANES))


def _mlstm_body(qkf_ref, vf_ref, gf_ref, gtf_ref, qkb_ref, vb_ref, gb_ref, gtb_ref,
                hf_ref, hb_ref, a_ref, m_ref):
    @pl.when(pl.program_id(1) == 0)
    def _():
        a_ref[...] = jnp.zeros_like(a_ref)
        m_ref[...] = jnp.zeros_like(m_ref)

    _mlstm_dir(qkf_ref, vf_ref, gf_ref, gtf_ref, hf_ref, a_ref, m_ref, backward=False)
    _mlstm_dir(qkb_ref, vb_ref, gb_ref, gtb_ref, hb_ref, a_ref, m_ref, backward=True)


def _mlstm(qk, v, gates, gates_t):
    B, S, _ = v.shape
    L = min(MLSTM_CHUNK, S)
    nc = S // L
    fwd = lambda width: pl.BlockSpec((None, L, width), lambda b, c: (b, c, 0))
    bwd = lambda width: pl.BlockSpec((None, L, width), lambda b, c: (b, nc - 1 - c, 0))
    return pl.pallas_call(
        _mlstm_body,
        grid=(B, nc),
        in_specs=[
            fwd(2 * M_W), fwd(M_W), fwd(LANES),
            pl.BlockSpec((None, N_GATES, L), lambda b, c: (b, 0, c)),
            bwd(2 * M_W), bwd(M_W), bwd(LANES),
            pl.BlockSpec((None, N_GATES, L), lambda b, c: (b, 0, nc - 1 - c)),
        ],
        out_specs=[fwd(M_W), bwd(M_W)],
        out_shape=[jax.ShapeDtypeStruct((B, S, M_W), BF16)] * 2,
        scratch_shapes=[
            pltpu.VMEM((2 * M_HEADS, M_HEAD_DIM, M_HEAD_DIM + LANES), F32),
            pltpu.VMEM((2 * M_HEADS, LANES), F32),
        ],
        compiler_params=_params("parallel", "arbitrary"),
        name="mlstm",
    )(qk, v, gates, gates_t, qk, v, gates, gates_t)


def _c_out_body(x_ref, hf_ref, hb_ref, o_ref, hg_ref, w_ref, g_ref, win_ref, wout_ref, out_ref):
    h = hf_ref[...].astype(F32) + hb_ref[...].astype(F32)
    normed = []
    for i in range(M_HEADS):
        hh = h[:, i * M_HEAD_DIM:(i + 1) * M_HEAD_DIM]
        ms = jnp.mean(hh * hh, axis=-1, keepdims=True)
        normed.append(hh * lax.rsqrt(ms + EPS))
    hn = jnp.concatenate(normed, axis=-1) * hg_ref[...]
    gated = (jax.nn.sigmoid(o_ref[...].astype(F32)) * hn).astype(BF16)
    x = x_ref[...] + _dot(gated, w_ref[...])
    out_ref[...] = _ffn_step(x, g_ref, win_ref, wout_ref)


def _c_out_ffn(x, hf, hb, o, hg, w, ffn):
    B, S, D = x.shape
    tm = min(TOKEN_TILE, S)
    tok = lambda width: pl.BlockSpec((None, tm, width), lambda b, s: (b, s, 0))
    return pl.pallas_call(
        _c_out_body,
        grid=(B, S // tm),
        in_specs=[tok(D), tok(M_W), tok(M_W), tok(M_W), _resident((1, M_W)), _resident(w.shape)] + _ffn_specs(),
        out_specs=tok(D),
        out_shape=jax.ShapeDtypeStruct((B, S, D), F32),
        compiler_params=_params("parallel", "parallel"),
        name="c_out_ffn",
    )(x, hf, hb, o, hg, w, *ffn)


def _rope_tables(S):
    t = jnp.arange(S, dtype=jnp.int32)
    row_id = (t // GRID_W).astype(F32)
    col_id = (t % GRID_W).astype(F32)
    freqs = ROPE_THETA ** (-jnp.arange(ROPE_FREQS, dtype=F32) / ROPE_FREQS)
    ang_r = row_id[:, None] * freqs[None, :]
    ang_c = col_id[:, None] * freqs[None, :]
    cos = jnp.concatenate([jnp.cos(ang_r)] * 2 + [jnp.cos(ang_c)] * 2, axis=-1)
    sin = jnp.concatenate([-jnp.sin(ang_r), jnp.sin(ang_r), -jnp.sin(ang_c), jnp.sin(ang_c)], axis=-1)
    return jnp.tile(cos, (1, LANES // HEAD_DIM)), jnp.tile(sin, (1, LANES // HEAD_DIM))


def _dft_matrices(n):
    j = jnp.arange(n, dtype=jnp.int32)
    phase = ((j[:, None] * j[None, :]) % n).astype(F32) * (2.0 * jnp.pi / n)
    return jnp.cos(phase), -jnp.sin(phase)


def _seq_dft_matrices(n):
    rows = min(DFT_PHASE_ROWS, n)
    t = jnp.arange(n, dtype=jnp.int32)
    unit = 2.0 * jnp.pi / n
    pa = ((jnp.arange(0, n, rows, dtype=jnp.int32)[:, None] * t[None, :]) % n).astype(F32) * unit
    pb = ((jnp.arange(rows, dtype=jnp.int32)[:, None] * t[None, :]) % n).astype(F32) * unit
    ca, sa = jnp.cos(pa)[:, None, :], jnp.sin(pa)[:, None, :]
    cb, sb = jnp.cos(pb)[None], jnp.sin(pb)[None]
    cmat = (ca * cb - sa * sb).reshape(n, n).astype(BF16)
    smat = (-(sa * cb + ca * sb)).reshape(n, n).astype(BF16)
    return cmat, smat


def _block_diag(block, copies):
    return jnp.kron(jnp.eye(copies, dtype=block.dtype), block)


def _prepare(ffn1_norm, ffn1_w_in, ffn1_w_out, mix_norm, ab_w_in, ab_q_norm, ab_k_norm, ab_w_out,
             c_w_in, c_gate_bias, c_conv, c_head_norm, c_w_out, ffn2_norm, ffn2_w_in, ffn2_w_out):
    groups = FOURIER_W // FOURIER_CH
    c64, s64 = _dft_matrices(FOURIER_CH)
    ch_scale = float(FOURIER_CH) ** -0.5
    shift = 1.01 * float(HEAD_DIM) ** 0.5 * jnp.max(jnp.abs(ab_q_norm[0])) * jnp.max(jnp.abs(ab_k_norm[0]))
    p = {
        "ffn1": [(ffn1_norm[l][None], ffn1_w_in[l].astype(BF16), ffn1_w_out[l].astype(BF16)) for l in range(2)],
        "ffn2": [(ffn2_norm[l][None], ffn2_w_in[l].astype(BF16), ffn2_w_out[l].astype(BF16)) for l in range(2)],
        "mix_norm": [mix_norm[l][None] for l in range(2)],
        "ab_w_in": ab_w_in[0].astype(BF16),
        "qg": jnp.tile(ab_q_norm[0], ATTN_W // HEAD_DIM)[None] * (float(HEAD_DIM) ** -0.5),
        "kg": jnp.tile(ab_k_norm[0], KV_W // HEAD_DIM)[None],
        "bd": _block_diag(jnp.full((HEAD_DIM, HEAD_DIM), 1.0 / HEAD_DIM, F32), KV_W // HEAD_DIM).astype(BF16),
        "dft64": jnp.concatenate([_block_diag(c64 * ch_scale, groups),
                                  _block_diag(-s64 * ch_scale, groups)], axis=-1).astype(BF16),
        "ab_wf": ab_w_out[0][:FOURIER_W].astype(BF16),
        "ab_wa": ab_w_out[0][FOURIER_W:].astype(BF16),
        "c_wqk": c_w_in[0][:, :2 * M_W].astype(BF16),
        "c_wvo": c_w_in[0][:, 2 * M_W:4 * M_W].astype(BF16),
        "c_wg": jnp.pad(c_w_in[0][:, 4 * M_W:], ((0, 0), (0, LANES - N_GATES))).astype(BF16),
        "c_wgt": c_w_in[0][:, 4 * M_W:].T.astype(BF16),
        "c_bias": jnp.pad(c_gate_bias[0], (0, LANES - N_GATES))[None],
        "c_biast": c_gate_bias[0][:, None],
        "c_conv": c_conv[0],
        "c_hg": c_head_norm[0][None],
        "shift": shift,
        "qpad": jnp.zeros((1, LANES), F32).at[0, HEAD_DIM].set(-shift),
        "kvpad": jnp.zeros((1, LANES), F32).at[0, HEAD_DIM].set(1.0),
        "c_wout": c_w_out[0].astype(BF16),
    }
    return p


def _trunk(x, p):
    B, S, _ = x.shape
    x = _ffn(x, *p["ffn1"][0])
    cos, sin = _rope_tables(S)
    q, k, v, y, z = _ab_in(x, p["mix_norm"][0], p["ab_w_in"], p["qg"], p["kg"], p["bd"], cos, sin, p["dft64"],
                           p["qpad"], p["kvpad"])
    cmat, smat = _seq_dft_matrices(S)
    fmix = _seq_dft(cmat, smat, y, z)
    attn = lax.cond(p["shift"] <= MAX_SOFTMAX_SHIFT,
                    functools.partial(_attention, online=False),
                    functools.partial(_attention, online=True), q, k, v)
    x = _ab_out_ffn(x, fmix, attn, p["ab_wf"], p["ab_wa"], p["ffn2"][0])
    x = _ffn(x, *p["ffn1"][1])
    qk, v, o, gates, gates_t = _c_in(x, p["mix_norm"][1], p["c_wqk"], p["c_wvo"], p["c_wg"], p["c_wgt"],
                                     p["c_bias"], p["c_biast"], p["c_conv"])
    hf, hb = _mlstm(qk, v, gates, gates_t)
    x = _c_out_ffn(x, hf, hb, o, p["c_hg"], p["c_wout"], p["ffn2"][1])
    return x


def kernel(x_prompt, x_sample, ffn1_norm, ffn1_w_in, ffn1_w_out, mix_norm, ab_w_in, ab_q_norm, ab_k_norm,
           ab_w_out, c_w_in, c_gate_bias, c_conv, c_head_norm, c_w_out, ffn2_norm, ffn2_w_in, ffn2_w_out):
    p = _prepare(ffn1_norm, ffn1_w_in, ffn1_w_out, mix_norm, ab_w_in, ab_q_norm, ab_k_norm, ab_w_out,
                 c_w_in, c_gate_bias, c_conv, c_head_norm, c_w_out, ffn2_norm, ffn2_w_in, ffn2_w_out)
    return (_trunk(x_prompt, p), _trunk(x_sample, p))
```
